```python
import math, functools
import jax, jax.numpy as jnp
from jax import lax
import numpy as np

D_MODEL = 1024
BATCH = 2
SEQ = 8192
DEPTH = 4
DEC_BATCH = 32
DEC_SEQ = 1
PAST_LEN = 8192
PAGE_SIZE = 128

M_HEADS = 4
M_DIM = 64
M_WIDTH = M_HEADS * M_DIM
CONV_W = 4
M_CHUNK = 128
D_HEADS = 4
D_QK = 64
D_V = 2 * D_QK
D_WIDTH = D_HEADS * D_V
F_HEADS = 4
F_DIM = 64
F_WIDTH = F_HEADS * F_DIM
MIX_WIDTH = M_WIDTH + D_WIDTH + F_WIDTH
Q_BLOCK = 128
EPS = 1e-6
SPLITS = (M_WIDTH, M_WIDTH, M_HEADS, M_HEADS, D_HEADS * 2 * D_QK, D_HEADS * 2 * D_QK, D_WIDTH,
          F_WIDTH, F_WIDTH, F_WIDTH, F_HEADS, MIX_WIDTH)
P_IN = sum(SPLITS)

kernel_name = 'hybrid_mlstm_diffattn_fox_step'


def rmsnorm(x, g):
    xf = x.astype(jnp.float32)
    y = xf * lax.rsqrt(jnp.mean(xf * xf, axis=-1, keepdims=True) + EPS)
    return (y * g.astype(jnp.float32)).astype(x.dtype)


def in_proj(xn, w_in):
    y = jnp.einsum('bsd,dp->bsp', xn, w_in)
    idx = [int(i) for i in np.cumsum(SPLITS)[:-1]]
    return jnp.split(y, idx, axis=-1)


def causal_conv(u_pad, w, b):
    L = u_pad.shape[1] - (CONV_W - 1)
    out = b
    for j in range(CONV_W):
        out = out + u_pad[:, j:j + L] * w[j]
    return out


def mlstm_chunk(q, k, v, li, lf, state):
    c_prev, n_prev, m_prev = state
    q, k, v = q.astype(jnp.float32), k.astype(jnp.float32), v.astype(jnp.float32)
    b = jnp.cumsum(lf, axis=1).transpose(0, 2, 1)
    li = li.transpose(0, 2, 1)
    L = b.shape[-1]
    causal = jnp.tril(jnp.ones((L, L), dtype=bool))
    log_w = jnp.where(causal, b[..., :, None] - b[..., None, :] + li[..., None, :], -jnp.inf)
    log_inter = b + m_prev[..., None]
    m_t = jnp.maximum(log_inter, jnp.max(log_w, axis=-1))
    w = jnp.exp(log_w - m_t[..., None])
    w_inter = jnp.exp(log_inter - m_t)
    s = jnp.einsum('bthd,bshd->bhts', q, k) * w
    num = (w_inter[..., None] * jnp.einsum('bthd,bhde->bhte', q, c_prev)
           + jnp.einsum('bhts,bshe->bhte', s, v))
    den = w_inter * jnp.einsum('bthd,bhd->bht', q, n_prev) + jnp.sum(s, axis=-1)
    h = num / jnp.maximum(jnp.abs(den), jnp.exp(-m_t))[..., None]
    w_last = w[..., -1, :]
    decay = w_inter[..., -1]
    c_new = decay[..., None, None] * c_prev + jnp.einsum('bhs,bshd,bshe->bhde', w_last, k, v)
    n_new = decay[..., None] * n_prev + jnp.einsum('bhs,bshd->bhd', w_last, k)
    return h.transpose(0, 2, 1, 3), (c_new, n_new, m_t[..., -1])


def mlstm_run(q, k, v, li, lf, state, chunk):
    B, S = q.shape[:2]
    nc = S // chunk
    to_chunks = lambda a: jnp.moveaxis(a.reshape((B, nc, chunk) + a.shape[2:]), 1, 0)

    def step(st, xs):
        h, st = mlstm_chunk(*xs, st)
        return st, h

    state = tuple(s.astype(jnp.float32) for s in state)
    st, hs = lax.scan(step, state, tuple(to_chunks(a) for a in (q, k, v, li, lf)))
    h = jnp.moveaxis(hs, 0, 1).reshape(B, S, M_HEADS, M_DIM)
    return h, st


def attn_probs(q, k, bias, mask):
    s = jnp.einsum('bqhd,bkhd->bhqk', q, k).astype(jnp.float32) * (q.shape[-1] ** -0.5)
    if bias is not None:
        s = s + bias
    s = jnp.where(mask, s, -jnp.inf)
    return jax.nn.softmax(s, axis=-1)


def diff_attend(q, k, v, lam, mask):
    p1 = attn_probs(q[..., 0, :], k[..., 0, :], None, mask)
    p2 = attn_probs(q[..., 1, :], k[..., 1, :], None, mask)
    return jnp.einsum('bhqk,bkhe->bqhe', p1 - lam * p2, v.astype(jnp.float32))


def fox_attend(q, k, v, f_q, f_k, mask):
    bias = f_q[..., :, None] - f_k[..., None, :]
    p = attn_probs(q, k, bias, mask)
    return jnp.einsum('bhqk,bkhe->bqhe', p, v.astype(jnp.float32))


def prompt_attend(dq, dk, dv, fq, fk, fv, flog, lam):
    B, S = dq.shape[:2]
    F = jnp.cumsum(flog, axis=1).transpose(0, 2, 1)
    kpos = jnp.arange(S)

    def block(i):
        start = i * Q_BLOCK
        qpos = start + jnp.arange(Q_BLOCK)
        mask = kpos[None, :] <= qpos[:, None]
        od = diff_attend(lax.dynamic_slice_in_dim(dq, start, Q_BLOCK, axis=1), dk, dv, lam, mask)
        of = fox_attend(lax.dynamic_slice_in_dim(fq, start, Q_BLOCK, axis=1), fk, fv,
                        lax.dynamic_slice_in_dim(F, start, Q_BLOCK, axis=2), F, mask)
        return od, of

    od, of = lax.map(block, jnp.arange(S // Q_BLOCK))
    unblock = lambda a: jnp.moveaxis(a, 0, 1).reshape((B, S) + a.shape[3:])
    return unblock(od), unblock(of)


def sample_attend(dq, dk, dv, fq, fk, fv, flog, lam, past_dk, past_dv, past_fk, past_fv, past_flog):
    nb, P = past_dk.shape[:2]
    sn = dq.shape[1]
    kd = jnp.concatenate([past_dk.reshape(nb, P, D_HEADS, 2, D_QK).astype(dk.dtype), dk], axis=1)
    vd = jnp.concatenate([past_dv.astype(dv.dtype), dv], axis=1)
    kf = jnp.concatenate([past_fk.astype(fk.dtype), fk], axis=1)
    vf = jnp.concatenate([past_fv.astype(fv.dtype), fv], axis=1)
    F = jnp.cumsum(jnp.concatenate([past_flog.astype(jnp.float32), flog], axis=1), axis=1).transpose(0, 2, 1)
    qpos = P + jnp.arange(sn)
    kpos = jnp.arange(P + sn)
    mask = kpos[None, :] <= qpos[:, None]
    od = diff_attend(dq, kd, vd, lam, mask)
    of = fox_attend(fq, kf, vf, F[..., P:], F, mask)
    return od, of


def gather_pages(cache, l, page_table):
    g = cache[l, page_table]
    return g.reshape((g.shape[0], g.shape[1] * g.shape[2]) + g.shape[3:])


def mixer_sublayer(x, conv_buf, m_state, attend, lam, lam_init, lw):
    (norm_pre, norm_post, w_in, conv_w, conv_b, wq_m, wk_m, b_i, b_f,
     m_norm, m_skip, d_norm, b_fox, w_out) = lw
    B, S = x.shape[:2]
    xn = rmsnorm(x, norm_pre)
    u_m, v_m, ig, fg, dq, dk, dv, fq, fk, fv, ffg, z = in_proj(xn, w_in)
    u_pad = jnp.concatenate([conv_buf.astype(u_m.dtype), u_m], axis=1)
    xc = jax.nn.silu(causal_conv(u_pad, conv_w, conv_b))
    xch = xc.reshape(B, S, M_HEADS, M_DIM)
    q_m = jnp.einsum('bshd,hde->bshe', xch, wq_m)
    k_m = jnp.einsum('bshd,hde->bshe', xch, wk_m) * (M_DIM ** -0.5)
    v_mh = v_m.reshape(B, S, M_HEADS, M_DIM)
    li = ig.astype(jnp.float32) + b_i.astype(jnp.float32)
    lf = jax.nn.log_sigmoid(fg.astype(jnp.float32) + b_f.astype(jnp.float32))
    chunk = M_CHUNK if S % M_CHUNK == 0 else S
    h_m, m_new = mlstm_run(q_m, k_m, v_mh, li, lf, m_state, chunk)
    h_m = rmsnorm(h_m, m_norm.reshape(M_HEADS, M_DIM)).reshape(B, S, M_WIDTH) + m_skip * xc
    dq = dq.reshape(B, S, D_HEADS, 2, D_QK)
    dk = dk.reshape(B, S, D_HEADS, 2, D_QK)
    dv = dv.reshape(B, S, D_HEADS, D_V)
    fq = fq.reshape(B, S, F_HEADS, F_DIM)
    fk = fk.reshape(B, S, F_HEADS, F_DIM)
    fv = fv.reshape(B, S, F_HEADS, F_DIM)
    flog = jax.nn.log_sigmoid(ffg.astype(jnp.float32) + b_fox.astype(jnp.float32))
    od, of = attend(dq, dk, dv, fq, fk, fv, flog, lam)
    od = rmsnorm(od, d_norm.reshape(D_HEADS, D_V)) * (1.0 - lam_init)
    h = jnp.concatenate([h_m.astype(jnp.float32), od.reshape(B, S, D_WIDTH),
                         of.reshape(B, S, F_WIDTH)], axis=-1) * jax.nn.silu(z.astype(jnp.float32))
    y = jnp.einsum('bsm,md->bsd', h.astype(x.dtype), w_out)
    x_new = x + rmsnorm(y, norm_post)
    new_state = (dk.reshape(B, S, D_HEADS, 2 * D_QK), dv, fk, fv, flog.astype(x.dtype),
                 u_pad[:, -(CONV_W - 1):], m_new[0], m_new[1], m_new[2])
    return x_new, new_state


def stack_field(states, i):
    return jnp.stack([st[i] for st in states])


def setup_inputs(seed: int = 0) -> dict:
    key = jax.random.key(seed)
    ks = iter(jax.random.split(key, 40))
    nrm = lambda shape, scale=1.0: jax.random.normal(next(ks), shape, jnp.float32) * scale
    n_pages = PAST_LEN // PAGE_SIZE
    n_used = DEC_BATCH * n_pages
    n_pool = n_used + max(1, n_used // 4)
    perm = jax.random.permutation(next(ks), n_pool)
    page_table = perm[:n_used].reshape(DEC_BATCH, n_pages).astype(jnp.int32)
    x_prompt = nrm((BATCH, SEQ, D_MODEL))
    x_sample = nrm((DEC_BATCH, DEC_SEQ, D_MODEL))
    cache_dk = nrm((DEPTH, n_pool, PAGE_SIZE, D_HEADS, 2 * D_QK))
    cache_dv = nrm((DEPTH, n_pool, PAGE_SIZE, D_HEADS, D_V))
    cache_fk = nrm((DEPTH, n_pool, PAGE_SIZE, F_HEADS, F_DIM))
    cache_fv = nrm((DEPTH, n_pool, PAGE_SIZE, F_HEADS, F_DIM))
    cache_flogf = jax.nn.log_sigmoid(3.0 + nrm((DEPTH, n_pool, PAGE_SIZE, F_HEADS)))
    state_mconv = nrm((DEPTH, DEC_BATCH, CONV_W - 1, M_WIDTH))
    state_mC = nrm((DEPTH, DEC_BATCH, M_HEADS, M_DIM, M_DIM), 0.1)
    state_mn = nrm((DEPTH, DEC_BATCH, M_HEADS, M_DIM), 0.1)
    state_mm = nrm((DEPTH, DEC_BATCH, M_HEADS), 0.5)
    return {
        'x_prompt': x_prompt, 'x_sample': x_sample,
        'cache_dk': cache_dk, 'cache_dv': cache_dv, 'cache_fk': cache_fk, 'cache_fv': cache_fv,
        'cache_flogf': cache_flogf,
        'state_mconv': state_mconv, 'state_mC': state_mC, 'state_mn': state_mn, 'state_mm': state_mm,
        'page_table': page_table,
        'norm_pre': 1.0 + nrm((DEPTH, D_MODEL), 0.02),
        'norm_post': 1.0 + nrm((DEPTH, D_MODEL), 0.02),
        'w_in': nrm((DEPTH, D_MODEL, P_IN), D_MODEL ** -0.5),
        'conv_w': nrm((DEPTH, CONV_W, M_WIDTH), CONV_W ** -0.5),
        'conv_b': nrm((DEPTH, M_WIDTH), 0.02),
        'wq_m': nrm((DEPTH, M_HEADS, M_DIM, M_DIM), M_DIM ** -0.5),
        'wk_m': nrm((DEPTH, M_HEADS, M_DIM, M_DIM), M_DIM ** -0.5),
        'b_i': nrm((DEPTH, M_HEADS), 0.1),
        'b_f': jnp.linspace(3.0, 6.0, M_HEADS)[None, :] + nrm((DEPTH, M_HEADS), 0.1),
        'm_norm': 1.0 + nrm((DEPTH, M_WIDTH), 0.02),
        'm_skip': 1.0 + nrm((DEPTH, M_WIDTH), 0.02),
        'lam_q1': nrm((DEPTH, D_QK), 0.1),
        'lam_k1': nrm((DEPTH, D_QK), 0.1),
        'lam_q2': nrm((DEPTH, D_QK), 0.1),
        'lam_k2': nrm((DEPTH, D_QK), 0.1),
        'd_norm': 1.0 + nrm((DEPTH, D_WIDTH), 0.02),
        'b_fox': jnp.linspace(1.0, 5.0, F_HEADS)[None, :] + nrm((DEPTH, F_HEADS), 0.1),
        'w_out': nrm((DEPTH, MIX_WIDTH, D_MODEL), MIX_WIDTH ** -0.5),
    }


def reference(x_prompt, x_sample, cache_dk, cache_dv, cache_fk, cache_fv, cache_flogf,
              state_mconv, state_mC, state_mn, state_mm, page_table,
              norm_pre, norm_post, w_in, conv_w, conv_b, wq_m, wk_m, b_i, b_f,
              m_norm, m_skip, lam_q1, lam_k1, lam_q2, lam_k2, d_norm, b_fox, w_out):
    xp, xs = x_prompt, x_sample
    bp = xp.shape[0]
    new_p, new_s = [], []
    for l in range(DEPTH):
        lam_init = 0.8 - 0.6 * math.exp(-0.3 * l)
        lam = (jnp.exp(jnp.sum(lam_q1[l].astype(jnp.float32) * lam_k1[l].astype(jnp.float32)))
               - jnp.exp(jnp.sum(lam_q2[l].astype(jnp.float32) * lam_k2[l].astype(jnp.float32)))
               + lam_init)
        lw = (norm_pre[l], norm_post[l], w_in[l], conv_w[l], conv_b[l], wq_m[l], wk_m[l], b_i[l], b_f[l],
              m_norm[l], m_skip[l], d_norm[l], b_fox[l], w_out[l])
        zero_conv = jnp.zeros((bp, CONV_W - 1, M_WIDTH), xp.dtype)
        zero_m = (jnp.zeros((bp, M_HEADS, M_DIM, M_DIM), jnp.float32),
                  jnp.zeros((bp, M_HEADS, M_DIM), jnp.float32),
                  jnp.zeros((bp, M_HEADS), jnp.float32))
        xp, st_p = mixer_sublayer(xp, zero_conv, zero_m, prompt_attend, lam, lam_init, lw)
        attend_s = functools.partial(
            sample_attend,
            past_dk=gather_pages(cache_dk, l, page_table),
            past_dv=gather_pages(cache_dv, l, page_table),
            past_fk=gather_pages(cache_fk, l, page_table),
            past_fv=gather_pages(cache_fv, l, page_table),
            past_flog=gather_pages(cache_flogf, l, page_table))
        xs, st_s = mixer_sublayer(xs, state_mconv[l], (state_mC[l], state_mn[l], state_mm[l]),
                                  attend_s, lam, lam_init, lw)
        new_p.append(st_p)
        new_s.append(st_s)
    p_dk, p_dv, p_fk, p_fv, p_flogf = (stack_field(new_p, 0), stack_field(new_p, 1), stack_field(new_p, 2),
                                       stack_field(new_p, 3), stack_field(new_p, 4))
    p_mconv, p_mC, p_mn, p_mm = (stack_field(new_p, 5), stack_field(new_p, 6), stack_field(new_p, 7),
                                 stack_field(new_p, 8))
    s_dk, s_dv, s_fk, s_fv, s_flogf = (stack_field(new_s, 0), stack_field(new_s, 1), stack_field(new_s, 2),
                                       stack_field(new_s, 3), stack_field(new_s, 4))
    s_mconv, s_mC, s_mn, s_mm = (stack_field(new_s, 5), stack_field(new_s, 6), stack_field(new_s, 7),
                                 stack_field(new_s, 8))
    return (xp, xs, p_dk, p_dv, p_fk, p_fv, p_flogf, p_mconv, p_mC, p_mn, p_mm,
            s_dk, s_dv, s_fk, s_fv, s_flogf, s_mconv, s_mC, s_mn, s_mm)
```

```python
import functools
import math

import jax
import jax.numpy as jnp
from jax import lax
from jax.experimental import pallas as pl
from jax.experimental.pallas import tpu as pltpu

D_MODEL = 1024
DEPTH = 4
PAGE_SIZE = 128
M_HEADS = 4
M_DIM = 64
M_WIDTH = M_HEADS * M_DIM
CONV_W = 4
M_CHUNK = 128
D_HEADS = 4
D_QK = 64
D_V = 2 * D_QK
D_WIDTH = D_HEADS * D_V
F_HEADS = 4
F_DIM = 64
F_WIDTH = F_HEADS * F_DIM
MIX_WIDTH = M_WIDTH + D_WIDTH + F_WIDTH
EPS = 1e-6
SPLITS = (M_WIDTH, M_WIDTH, M_HEADS, M_HEADS, D_WIDTH, D_WIDTH, D_WIDTH,
          F_WIDTH, F_WIDTH, F_WIDTH, F_HEADS, MIX_WIDTH)

LANES = 128
GATE_ROWS = 16
VMEM_LIMIT = 56 * 1024 * 1024

F32 = jnp.float32
BF16 = jnp.bfloat16
NEG_INF = float("-inf")

_MAIN = (("um", M_WIDTH), ("vm", M_WIDTH), ("dq", D_WIDTH), ("dk", D_WIDTH), ("dv", D_WIDTH),
         ("fq", F_WIDTH), ("fk", F_WIDTH), ("fv", F_WIDTH), ("z", MIX_WIDTH))
_MAIN_OFF = {}
_o = 0
for _n, _w in _MAIN:
    _MAIN_OFF[_n] = (_o, _w)
    _o += _w
MAIN_WIDTH = _o


def _cparams(sem):
    return pltpu.CompilerParams(dimension_semantics=sem, vmem_limit_bytes=VMEM_LIMIT)


def _log_sigmoid(x):
    return jnp.minimum(x, 0.0) - jnp.log1p(jnp.exp(-jnp.abs(x)))


def _silu(x):
    return x * (1.0 / (1.0 + jnp.exp(-x)))


def _dot(a, b):
    return jnp.dot(a, b, preferred_element_type=F32)


def _dot_nt(a, b):
    return lax.dot_general(a, b, (((1,), (1,)), ((), ())), preferred_element_type=F32)


def _dot_tn(a, b):
    return lax.dot_general(a, b, (((0,), (0,)), ((), ())), preferred_element_type=F32)


def _in_proj_kernel(x_ref, g_ref, w_ref, wg_ref, gb_ref, tri_ref,
                    um_ref, vm_ref, dq_ref, dk32_ref, dk16_ref, dv32_ref, dv16_ref,
                    fq_ref, fk32_ref, fk16_ref, fv32_ref, fv16_ref, z_ref, gates_ref,
                    carry_ref, *, tiles_per_seq):
    i = pl.program_id(0)
    x = x_ref[...]
    xn = x * lax.rsqrt(jnp.mean(x * x, axis=-1, keepdims=True) + EPS) * g_ref[...]
    xb = xn.astype(BF16)

    def proj(name):
        off, width = _MAIN_OFF[name]
        return _dot(xb, w_ref[:, off:off + width])

    um_ref[...] = proj("um")
    vm_ref[...] = proj("vm")
    dq_ref[...] = proj("dq").astype(BF16)
    dk = proj("dk")
    dk32_ref[...] = dk
    dk16_ref[...] = dk.astype(BF16)
    dv = proj("dv")
    dv32_ref[...] = dv
    dv16_ref[...] = dv.astype(BF16)
    fq_ref[...] = proj("fq").astype(BF16)
    fk = proj("fk")
    fk32_ref[...] = fk
    fk16_ref[...] = fk.astype(BF16)
    fv = proj("fv")
    fv32_ref[...] = fv
    fv16_ref[...] = fv.astype(BF16)
    z_ref[...] = proj("z")

    pre = _dot_nt(wg_ref[...], xb) + gb_ref[...]
    row = lax.broadcasted_iota(jnp.int32, pre.shape, 0)
    act = jnp.where(row < M_HEADS, pre, _log_sigmoid(pre))

    @pl.when(i % tiles_per_seq == 0)
    def _():
        carry_ref[...] = jnp.zeros_like(carry_ref)

    hi = act.astype(BF16)
    r1 = act - hi.astype(F32)
    mid = r1.astype(BF16)
    lo = (r1 - mid.astype(F32)).astype(BF16)
    tri = tri_ref[...]
    cs = _dot(hi, tri) + _dot(mid, tri) + _dot(lo, tri) + carry_ref[:, 0:1]
    carry_ref[...] = jnp.broadcast_to(cs[:, cs.shape[1] - 1:], carry_ref.shape)
    shifted = pltpu.roll(cs, 4, axis=0)
    gates_ref[...] = jnp.where((row >= 12), shifted, jnp.where(row < 12, act, 0.0))


def _in_proj(x2d, norm_g, w_main, w_gate_t, gate_bias, tri, tm, tiles_per_seq):
    n = x2d.shape[0]
    grid = (n // tm,)
    row_spec = lambda w: pl.BlockSpec((tm, w), lambda i: (i, 0))
    const = lambda shape: pl.BlockSpec(shape, lambda i: (0,) * len(shape))
    out_shape = [
        jax.ShapeDtypeStruct((n, M_WIDTH), F32),
        jax.ShapeDtypeStruct((n, M_WIDTH), F32),
        jax.ShapeDtypeStruct((n, D_WIDTH), BF16),
        jax.ShapeDtypeStruct((n, D_WIDTH), F32),
        jax.ShapeDtypeStruct((n, D_WIDTH), BF16),
        jax.ShapeDtypeStruct((n, D_WIDTH), F32),
        jax.ShapeDtypeStruct((n, D_WIDTH), BF16),
        jax.ShapeDtypeStruct((n, F_WIDTH), BF16),
        jax.ShapeDtypeStruct((n, F_WIDTH), F32),
        jax.ShapeDtypeStruct((n, F_WIDTH), BF16),
        jax.ShapeDtypeStruct((n, F_WIDTH), F32),
        jax.ShapeDtypeStruct((n, F_WIDTH), BF16),
        jax.ShapeDtypeStruct((n, MIX_WIDTH), F32),
        jax.ShapeDtypeStruct((GATE_ROWS, n), F32),
    ]
    out_specs = [row_spec(s.shape[1]) for s in out_shape[:-1]]
    out_specs.append(pl.BlockSpec((GATE_ROWS, tm), lambda i: (0, i)))
    return pl.pallas_call(
        functools.partial(_in_proj_kernel, tiles_per_seq=tiles_per_seq),
        grid=grid,
        in_specs=[row_spec(D_MODEL), const((1, D_MODEL)), const((D_MODEL, MAIN_WIDTH)),
                  const((GATE_ROWS, D_MODEL)), const((GATE_ROWS, 1)), const((tm, tm))],
        out_specs=out_specs,
        out_shape=out_shape,
        scratch_shapes=[pltpu.VMEM((GATE_ROWS, LANES), F32)],
        compiler_params=_cparams(("arbitrary",)),
        name="in_proj",
    )(x2d, norm_g, w_main, w_gate_t, gate_bias, tri)


def _mlstm_kernel(um_ref, vm_ref, gates_ref, hist_ref, c0_ref, n0_ref, m0_ref,
                  cw_ref, cb_ref, wq_ref, wk_ref, mnorm_ref, mskip_ref,
                  hm_ref, c_out_ref, n_out_ref, m_out_ref,
                  ubuf, c_s, n_s, m_s):
    c = pl.program_id(1)
    L = M_CHUNK

    @pl.when(c == 0)
    def _():
        ubuf[0:8, :] = hist_ref[0]
        c_s[...] = c0_ref[0]
        n_s[...] = n0_ref[0]
        m_s[...] = m0_ref[0]

    ubuf[8:8 + L, :] = um_ref[...]
    conv = cb_ref[...]
    for j in range(CONV_W):
        conv = conv + ubuf[5 + j:5 + j + L, :] * cw_ref[j:j + 1, :]
    ubuf[0:8, :] = ubuf[L:L + 8, :]
    xc = _silu(conv)
    xb = xc.astype(BF16)
    q_all = _dot(xb, wq_ref[...])
    k_all = _dot(xb, wk_ref[...])
    v_all = vm_ref[...]

    gates = gates_ref[...]
    lane = lax.broadcasted_iota(jnp.int32, gates.shape, 1)
    bcum = gates
    shift = 1
    while shift < L:
        bcum = bcum + jnp.where(lane >= shift, pltpu.roll(bcum, shift, axis=1), 0.0)
        shift *= 2

    ri = lax.broadcasted_iota(jnp.int32, (L, L), 0)
    ci = lax.broadcasted_iota(jnp.int32, (L, L), 1)
    causal = ci <= ri
    eye = ci == ri

    outs = []
    m_prev_all = m_s[...]
    m_new_all = m_prev_all
    lane1 = lax.broadcasted_iota(jnp.int32, m_prev_all.shape, 1)
    for h in range(M_HEADS):
        sl = slice(h * M_DIM, (h + 1) * M_DIM)
        q_h, k_h, v_h = q_all[:, sl], k_all[:, sl], v_all[:, sl]
        li_row = gates[h:h + 1, :]
        lf_row = gates[M_HEADS + h:M_HEADS + h + 1, :]
        a_row = li_row - bcum[M_HEADS + h:M_HEADS + h + 1, :]
        m_prev = m_prev_all[:, h:h + 1]
        c_prev = c_s[h]
        n_prev = n_s[h:h + 1, :]

        a_mat = jnp.where(causal, a_row, NEG_INF)
        g = jnp.maximum(jnp.max(a_mat, axis=1, keepdims=True), m_prev)
        b_col = jnp.sum(jnp.where(causal, lf_row, 0.0), axis=1, keepdims=True)
        w = jnp.exp(a_mat - g)
        w_inter = jnp.exp(m_prev - g)
        qb = q_h.astype(BF16)
        s = _dot_nt(qb, k_h.astype(BF16)) * w
        num = w_inter * _dot(qb, c_prev.astype(BF16)) + _dot(s.astype(BF16), v_h.astype(BF16))
        den = (w_inter * jnp.sum(q_h * n_prev, axis=1, keepdims=True)
               + jnp.sum(s, axis=1, keepdims=True))
        m_t = b_col + g
        hh = num / jnp.maximum(jnp.abs(den), jnp.exp(-m_t))
        hn = hh * lax.rsqrt(jnp.mean(hh * hh, axis=1, keepdims=True) + EPS)
        outs.append(hn * mnorm_ref[:, sl] + mskip_ref[:, sl] * xc[:, sl])

        g_last = g[L - 1:L, :]
        decay = jnp.exp(m_prev - g_last)
        w_last_row = w[L - 1:L, :]
        w_last_col = jnp.sum(jnp.where(eye, w_last_row, 0.0), axis=1, keepdims=True)
        kw = k_h * w_last_col
        c_s[h] = decay * c_prev + _dot_tn(kw.astype(BF16), v_h.astype(BF16))
        n_s[h:h + 1, :] = decay * n_prev + jnp.sum(kw, axis=0, keepdims=True)
        m_new_all = jnp.where(lane1 == h, m_t[L - 1:L, :], m_new_all)
    m_s[...] = m_new_all
    hm_ref[...] = jnp.concatenate(outs, axis=1)

    @pl.when(c == pl.num_programs(1) - 1)
    def _():
        c_out_ref[0] = c_s[...]
        n_out_ref[0] = n_s[...]
        m_out_ref[0] = m_s[...]


def _mlstm(um, vm, gates, hist8, c0, n0, m0, cw, cb, wq_bd, wk_bd, mnorm, mskip, n_seq, n_chunks):
    n = um.shape[0]
    L = M_CHUNK
    tok = pl.BlockSpec((L, M_WIDTH), lambda b, c: (b * n_chunks + c, 0))
    const = lambda shape: pl.BlockSpec(shape, lambda b, c: (0,) * len(shape))
    per_seq = lambda shape: pl.BlockSpec((1,) + shape, lambda b, c: (b,) + (0,) * len(shape))
    return pl.pallas_call(
        _mlstm_kernel,
        grid=(n_seq, n_chunks),
        in_specs=[tok, tok,
                  pl.BlockSpec((GATE_ROWS, L), lambda b, c: (0, b * n_chunks + c)),
                  per_seq((8, M_WIDTH)), per_seq((M_HEADS, M_DIM, M_DIM)),
                  per_seq((M_HEADS, M_DIM)), per_seq((1, LANES)),
                  const((CONV_W, M_WIDTH)), const((1, M_WIDTH)),
                  const((M_WIDTH, M_WIDTH)), const((M_WIDTH, M_WIDTH)),
                  const((1, M_WIDTH)), const((1, M_WIDTH))],
        out_specs=[tok, per_seq((M_HEADS, M_DIM, M_DIM)), per_seq((M_HEADS, M_DIM)),
                   per_seq((1, LANES))],
        out_shape=[jax.ShapeDtypeStruct((n, M_WIDTH), F32),
                   jax.ShapeDtypeStruct((n_seq, M_HEADS, M_DIM, M_DIM), F32),
                   jax.ShapeDtypeStruct((n_seq, M_HEADS, M_DIM), F32),
                   jax.ShapeDtypeStruct((n_seq, 1, LANES), F32)],
        scratch_shapes=[pltpu.VMEM((L + 8, M_WIDTH), F32),
                        pltpu.VMEM((M_HEADS, M_DIM, M_DIM), F32),
                        pltpu.VMEM((M_HEADS, M_DIM), F32),
                        pltpu.VMEM((1, LANES), F32)],
        compiler_params=_cparams(("arbitrary", "arbitrary")),
        name="mlstm",
    )(um, vm, gates, hist8, c0, n0, m0, cw, cb, wq_bd, wk_bd, mnorm, mskip)


def _lambda(lq1_ref, lk1_ref, lq2_ref, lk2_ref, lam_init):
    e1 = jnp.exp(jnp.sum(lq1_ref[...] * lk1_ref[...], axis=1, keepdims=True))
    e2 = jnp.exp(jnp.sum(lq2_ref[...] * lk2_ref[...], axis=1, keepdims=True))
    return e1 - e2 + lam_init


def _attn_kernel(*refs, tq, tk, fox, lam_init):
    if fox:
        q_ref, k_ref, v_ref, f_ref, o_ref, qs, m_s, l_s, acc = refs
    else:
        (q_ref, k_ref, v_ref, dn_ref, lq1_ref, lk1_ref, lq2_ref, lk2_ref,
         o_ref, qs, m_s, l_s, acc) = refs
    i = pl.program_id(2)
    j = pl.program_id(3)
    j_last = ((i + 1) * tq - 1) // tk
    half = LANES // 2
    f_row = 12 + 2 * pl.program_id(1)

    @pl.when(j == 0)
    def _():
        q = q_ref[...].astype(F32)
        lane = lax.broadcasted_iota(jnp.int32, q.shape, 1)
        qs[0:tq, :] = jnp.where(lane < half, q, 0.0).astype(BF16)
        qs[tq:2 * tq, :] = jnp.where(lane >= half, q, 0.0).astype(BF16)
        m_s[...] = jnp.full_like(m_s, NEG_INF)
        l_s[...] = jnp.zeros_like(l_s)
        acc[...] = jnp.zeros_like(acc)

    def step(masked):
        s = _dot_nt(qs[...], k_ref[...])
        if fox:
            f0 = f_ref[pl.ds(f_row, 1), :]
            f1 = f_ref[pl.ds(f_row + 1, 1), :]
            row = lax.broadcasted_iota(jnp.int32, s.shape, 0)
            s = s - jnp.where(row < tq, f0, f1)
        if masked:
            row = lax.broadcasted_iota(jnp.int32, s.shape, 0)
            col = lax.broadcasted_iota(jnp.int32, s.shape, 1)
            qpos = i * tq + jnp.where(row < tq, row, row - tq)
            s = jnp.where(j * tk + col <= qpos, s, NEG_INF)
        m_prev = m_s[...]
        m_new = jnp.maximum(m_prev, jnp.max(s, axis=1, keepdims=True))
        alpha = jnp.exp(m_prev - m_new)
        p = jnp.exp(s - m_new)
        l_s[...] = alpha * l_s[...] + jnp.sum(p, axis=1, keepdims=True)
        acc[...] = alpha * acc[...] + _dot(p.astype(BF16), v_ref[...])
        m_s[...] = m_new

    @pl.when(j < j_last)
    def _():
        step(False)

    @pl.when(j == j_last)
    def _():
        step(True)
        o = acc[...] / l_s[...]
        top, bot = o[0:tq, :], o[tq:2 * tq, :]
        if fox:
            lane = lax.broadcasted_iota(jnp.int32, top.shape, 1)
            o_ref[...] = jnp.where(lane < half, top, bot)
        else:
            lam = _lambda(lq1_ref, lk1_ref, lq2_ref, lk2_ref, lam_init)
            od = top - lam * bot
            od = od * lax.rsqrt(jnp.mean(od * od, axis=1, keepdims=True) + EPS)
            o_ref[...] = od * dn_ref[...] * (1.0 - lam_init)


def _prompt_attn(q, k, v, extra, n_seq, seq, tq, tk, fox, lam_init):
    n, width = q.shape
    groups = width // LANES
    nq, nk = seq // tq, seq // tk
    jl = lambda i: ((i + 1) * tq - 1) // tk
    q_spec = pl.BlockSpec((tq, LANES), lambda b, h, i, j: (b * nq + i, h))
    kv_spec = pl.BlockSpec((tk, LANES), lambda b, h, i, j: (b * nk + jnp.minimum(j, jl(i)), h))
    if fox:
        extra_specs = [pl.BlockSpec((GATE_ROWS, tk),
                                    lambda b, h, i, j: (0, b * nk + jnp.minimum(j, jl(i))))]
    else:
        lam_spec = pl.BlockSpec((1, D_QK), lambda b, h, i, j: (0, 0))
        extra_specs = [pl.BlockSpec((1, LANES), lambda b, h, i, j: (0, h))] + [lam_spec] * 4
    return pl.pallas_call(
        functools.partial(_attn_kernel, tq=tq, tk=tk, fox=fox, lam_init=lam_init),
        grid=(n_seq, groups, nq, nk),
        in_specs=[q_spec, kv_spec, kv_spec] + extra_specs,
        out_specs=q_spec,
        out_shape=jax.ShapeDtypeStruct((n, width), F32),
        scratch_shapes=[pltpu.VMEM((2 * tq, LANES), BF16),
                        pltpu.VMEM((2 * tq, 1), F32),
                        pltpu.VMEM((2 * tq, 1), F32),
                        pltpu.VMEM((2 * tq, LANES), F32)],
        compiler_params=_cparams(("arbitrary",) * 4),
        name="fox_attn" if fox else "diff_attn",
    )(q, k, v, *extra)


def _decode_kernel(pt_ref, *refs, pages_per_step, lam_init):
    G = pages_per_step
    (dq_ref, fq_ref, dkn_ref, dvn_ref, fkn_ref, fvn_ref, gn_ref,
     dn_ref, lq1_ref, lk1_ref, lq2_ref, lk2_ref) = refs[:12]
    page_refs = refs[12:12 + 5 * G]
    od_ref, of_ref = refs[12 + 5 * G:14 + 5 * G]
    qd_s, qf_s, md_s, ld_s, accd, mf_s, lf_s, accf, fcar = refs[14 + 5 * G:]
    j = pl.program_id(1)

    def group_mask(shape):
        r = lax.broadcasted_iota(jnp.int32, shape, 0)
        c = lax.broadcasted_iota(jnp.int32, shape, 1)
        return lax.shift_right_logical(c, 6) == r

    @pl.when(j == 0)
    def _():
        dq = jnp.broadcast_to(dq_ref[0].astype(F32), (8, D_WIDTH))
        qd_s[...] = jnp.where(group_mask(dq.shape), dq, 0.0).astype(BF16)
        fq = jnp.broadcast_to(fq_ref[0].astype(F32), (8, F_WIDTH))
        qf_s[...] = jnp.where(group_mask(fq.shape), fq, 0.0).astype(BF16)
        md_s[...] = jnp.full_like(md_s, NEG_INF)
        mf_s[...] = jnp.full_like(mf_s, NEG_INF)
        ld_s[...] = jnp.zeros_like(ld_s)
        lf_s[...] = jnp.zeros_like(lf_s)
        accd[...] = jnp.zeros_like(accd)
        accf[...] = jnp.zeros_like(accf)
        fcar[...] = jnp.zeros_like(fcar)

    def online(s, v, m_ref, l_ref, acc_ref):
        m_prev = m_ref[...]
        m_new = jnp.maximum(m_prev, jnp.max(s, axis=1, keepdims=True))
        alpha = jnp.exp(m_prev - m_new)
        p = jnp.exp(s - m_new)
        l_ref[...] = alpha * l_ref[...] + jnp.sum(p, axis=1, keepdims=True)
        acc_ref[...] = alpha * acc_ref[...] + _dot(p.astype(BF16), v)
        m_ref[...] = m_new

    lane = lax.broadcasted_iota(jnp.int32, (8, PAGE_SIZE), 1)
    for g in range(G):
        kd_ref, vd_ref, kf_ref, vf_ref, fl_ref = page_refs[5 * g:5 * g + 5]
        s_d = _dot_nt(qd_s[...], kd_ref[0, 0].astype(BF16))
        online(s_d, vd_ref[0, 0].astype(BF16), md_s, ld_s, accd)
        cum = fl_ref[0, 0]
        shift = 1
        while shift < PAGE_SIZE:
            cum = cum + jnp.where(lane >= shift, pltpu.roll(cum, shift, axis=1), 0.0)
            shift *= 2
        cum = cum + fcar[...]
        fcar[...] = cum[:, PAGE_SIZE - 1:]
        s_f = _dot_nt(qf_s[...], kf_ref[0, 0].astype(BF16)) - cum
        online(s_f, vf_ref[0, 0].astype(BF16), mf_s, lf_s, accf)

    @pl.when(j == pl.num_programs(1) - 1)
    def _():
        def add_new(q_s, k_new, v_new, bias, m_ref, l_ref, acc_ref):
            s = jnp.sum(q_s[...].astype(F32) * k_new, axis=1, keepdims=True) - bias
            m_prev = m_ref[...]
            m_new = jnp.maximum(m_prev, s)
            alpha = jnp.exp(m_prev - m_new)
            p = jnp.exp(s - m_new)
            l_new = alpha * l_ref[...] + p
            return (alpha * acc_ref[...] + p * v_new) / l_new

        od_all = add_new(qd_s, dkn_ref[0], dvn_ref[0], 0.0, md_s, ld_s, accd)
        f_new = fcar[...] + gn_ref[0]
        of_all = add_new(qf_s, fkn_ref[0], fvn_ref[0], f_new, mf_s, lf_s, accf)

        lam = _lambda(lq1_ref, lk1_ref, lq2_ref, lk2_ref, lam_init)
        r = lax.broadcasted_iota(jnp.int32, od_all.shape, 0)
        c = lax.broadcasted_iota(jnp.int32, od_all.shape, 1)
        head = lax.shift_right_logical(c, 7)
        od = jnp.sum(jnp.where(r == 2 * head, od_all, 0.0)
                     - lam * jnp.where(r == 2 * head + 1, od_all, 0.0), axis=0, keepdims=True)
        parts = []
        for h in range(D_HEADS):
            seg = od[:, h * D_V:(h + 1) * D_V]
            parts.append(seg * lax.rsqrt(jnp.mean(seg * seg, axis=1, keepdims=True) + EPS))
        od_ref[0] = jnp.concatenate(parts, axis=1) * dn_ref[...] * (1.0 - lam_init)
        of_ref[0] = jnp.sum(jnp.where(group_mask(of_all.shape), of_all, 0.0), axis=0, keepdims=True)


def _decode_attn(page_table, layer, dq, fq, dk_new, dv_new, fk_new, fv_new, g_new,
                 d_norm, lq1, lk1, lq2, lk2, c_dk, c_dv, c_fk, c_fv, c_flt,
                 pages_per_step, lam_init):
    nb, n_pages = page_table.shape
    G = pages_per_step
    steps = n_pages // G
    per_b = lambda w: pl.BlockSpec((1, 1, w), lambda b, j, pt: (b, 0, 0))
    const = lambda shape: pl.BlockSpec(shape, lambda b, j, pt: (0,) * len(shape))
    in_specs = [per_b(D_WIDTH), per_b(F_WIDTH), per_b(D_WIDTH), per_b(D_WIDTH),
                per_b(F_WIDTH), per_b(F_WIDTH),
                pl.BlockSpec((1, 8, 1), lambda b, j, pt: (b, 0, 0)),
                const((1, D_WIDTH))] + [const((1, D_QK))] * 4
    args = [dq, fq, dk_new, dv_new, fk_new, fv_new, g_new, d_norm, lq1, lk1, lq2, lk2]
    for g in range(G):
        def page(rows, w, g=g):
            return pl.BlockSpec((1, 1, rows, w),
                                lambda b, j, pt: (layer, pt[b, j * G + g], 0, 0))
        in_specs += [page(PAGE_SIZE, D_WIDTH), page(PAGE_SIZE, D_WIDTH),
                     page(PAGE_SIZE, F_WIDTH), page(PAGE_SIZE, F_WIDTH), page(8, PAGE_SIZE)]
        args += [c_dk, c_dv, c_fk, c_fv, c_flt]
    grid_spec = pltpu.PrefetchScalarGridSpec(
        num_scalar_prefetch=1,
        grid=(nb, steps),
        in_specs=in_specs,
        out_specs=[per_b(D_WIDTH), per_b(F_WIDTH)],
        scratch_shapes=[pltpu.VMEM((8, D_WIDTH), BF16), pltpu.VMEM((8, F_WIDTH), BF16),
                        pltpu.VMEM((8, 1), F32), pltpu.VMEM((8, 1), F32),
                        pltpu.VMEM((8, D_WIDTH), F32),
                        pltpu.VMEM((8, 1), F32), pltpu.VMEM((8, 1), F32),
                        pltpu.VMEM((8, F_WIDTH), F32),
                        pltpu.VMEM((8, 1), F32)])
    return pl.pallas_call(
        functools.partial(_decode_kernel, pages_per_step=G, lam_init=lam_init),
        grid_spec=grid_spec,
        out_shape=[jax.ShapeDtypeStruct((nb, 1, D_WIDTH), F32),
                   jax.ShapeDtypeStruct((nb, 1, F_WIDTH), F32)],
        compiler_params=_cparams(("arbitrary", "arbitrary")),
        name="decode_attn",
    )(page_table, *args)


def _out_proj_kernel(x_ref, hm_ref, od_ref, of_ref, z_ref, w_ref, g_ref, o_ref):
    h = jnp.concatenate([hm_ref[...], od_ref[...], of_ref[...]], axis=1) * _silu(z_ref[...])
    y = _dot(h.astype(BF16), w_ref[...])
    yn = y * lax.rsqrt(jnp.mean(y * y, axis=-1, keepdims=True) + EPS) * g_ref[...]
    o_ref[...] = x_ref[...] + yn


def _out_proj(x2d, hm, od, of, z, w_out, norm_g, tm):
    n = x2d.shape[0]
    row_spec = lambda w: pl.BlockSpec((tm, w), lambda i: (i, 0))
    const = lambda shape: pl.BlockSpec(shape, lambda i: (0,) * len(shape))
    return pl.pallas_call(
        _out_proj_kernel,
        grid=(n // tm,),
        in_specs=[row_spec(D_MODEL), row_spec(M_WIDTH), row_spec(D_WIDTH), row_spec(F_WIDTH),
                  row_spec(MIX_WIDTH), const((MIX_WIDTH, D_MODEL)), const((1, D_MODEL))],
        out_specs=row_spec(D_MODEL),
        out_shape=jax.ShapeDtypeStruct((n, D_MODEL), F32),
        compiler_params=_cparams(("arbitrary",)),
        name="out_proj",
    )(x2d, hm, od, of, z, w_out, norm_g)


def _block_diag(w):
    hh, d, _ = w.shape
    out = jnp.zeros((hh * d, hh * d), w.dtype)
    for h in range(hh):
        out = out.at[h * d:(h + 1) * d, h * d:(h + 1) * d].set(w[h])
    return out


def _layer_weights(l, w_in, b_i, b_f, b_fox, wq_m, wk_m):
    idx = [0]
    for s in SPLITS:
        idx.append(idx[-1] + s)
    cols = lambda k: w_in[l][:, idx[k]:idx[k + 1]]
    u_m, v_m, ig, fg, dq, dk, dv, fq, fk, fv, ffg, z = [cols(k) for k in range(len(SPLITS))]
    w_main = jnp.concatenate([u_m, v_m, dq * (D_QK ** -0.5), dk, dv, fq * (F_DIM ** -0.5), fk, fv, z],
                             axis=1).astype(BF16)
    w_gate_t = jnp.concatenate([ig, fg, ffg, jnp.zeros((D_MODEL, GATE_ROWS - 12), F32)],
                               axis=1).T.astype(BF16)
    gate_bias = jnp.concatenate([b_i[l], b_f[l], b_fox[l], jnp.zeros((GATE_ROWS - 12,), F32)]
                                ).astype(F32).reshape(GATE_ROWS, 1)
    wq_bd = _block_diag(wq_m[l]).astype(BF16)
    wk_bd = (_block_diag(wk_m[l]) * (M_DIM ** -0.5)).astype(BF16)
    return w_main, w_gate_t, gate_bias, wq_bd, wk_bd


def kernel(x_prompt, x_sample, cache_dk, cache_dv, cache_fk, cache_fv, cache_flogf, state_mconv, state_mC, state_mn, state_mm, page_table, norm_pre, norm_post, w_in, conv_w, conv_b, wq_m, wk_m, b_i, b_f, m_norm, m_skip, lam_q1, lam_k1, lam_q2, lam_k2, d_norm, b_fox, w_out):
    bp, seq, _ = x_prompt.shape
    nb = x_sample.shape[0]
    n_pool = cache_dk.shape[1]
    n_p = bp * seq
    tm_p = 512
    tq, tk = 256, 512
    pages_per_step = 4
    L = M_CHUNK

    xp = x_prompt.reshape(n_p, D_MODEL)
    ns = LANES
    xs = jnp.pad(x_sample.reshape(nb, D_MODEL), ((0, ns - nb), (0, 0)))
    c_dk = cache_dk.reshape(DEPTH, n_pool, PAGE_SIZE, D_WIDTH)
    c_dv = cache_dv.reshape(DEPTH, n_pool, PAGE_SIZE, D_WIDTH)
    c_fk = cache_fk.reshape(DEPTH, n_pool, PAGE_SIZE, F_WIDTH)
    c_fv = cache_fv.reshape(DEPTH, n_pool, PAGE_SIZE, F_WIDTH)
    c_flt = jnp.pad(jnp.swapaxes(cache_flogf.astype(F32), 2, 3), ((0, 0), (0, 0), (0, 8 - F_HEADS), (0, 0)))

    tri_p = (lax.broadcasted_iota(jnp.int32, (tm_p, tm_p), 0)
             <= lax.broadcasted_iota(jnp.int32, (tm_p, tm_p), 1)).astype(BF16)
    tri_s = jnp.eye(ns, dtype=BF16)

    zero_hist = jnp.zeros((bp, 8, M_WIDTH), F32)
    zero_c = jnp.zeros((bp, M_HEADS, M_DIM, M_DIM), F32)
    zero_n = jnp.zeros((bp, M_HEADS, M_DIM), F32)
    zero_m = jnp.zeros((bp, 1, LANES), F32)

    new_p, new_s = [], []
    for l in range(DEPTH):
        lam_init = 0.8 - 0.6 * math.exp(-0.3 * l)
        w_main, w_gate_t, gate_bias, wq_bd, wk_bd = _layer_weights(l, w_in, b_i, b_f, b_fox, wq_m, wk_m)
        w_out_l = w_out[l].astype(BF16)
        g_pre = norm_pre[l].reshape(1, D_MODEL)
        g_post = norm_post[l].reshape(1, D_MODEL)
        cw, cb = conv_w[l], conv_b[l].reshape(1, M_WIDTH)
        mnorm, mskip = m_norm[l].reshape(1, M_WIDTH), m_skip[l].reshape(1, M_WIDTH)
        dn = d_norm[l].reshape(1, D_WIDTH)
        lams = [a[l].reshape(1, D_QK).astype(F32) for a in (lam_q1, lam_k1, lam_q2, lam_k2)]

        (um, vm, dq, dk32, dk16, dv32, dv16, fq, fk32, fk16, fv32, fv16, z, gates) = _in_proj(
            xp, g_pre, w_main, w_gate_t, gate_bias, tri_p, tm_p, seq // tm_p)
        hm, c_fin, n_fin, m_fin = _mlstm(um, vm, gates, zero_hist, zero_c, zero_n, zero_m,
                                         cw, cb, wq_bd, wk_bd, mnorm, mskip, bp, seq // L)
        od = _prompt_attn(dq, dk16, dv16, [dn] + lams, bp, seq, tq, tk, False, lam_init)
        of = _prompt_attn(fq, fk16, fv16, [gates], bp, seq, tq, tk, True, lam_init)
        xp = _out_proj(xp, hm, od, of, z, w_out_l, g_post, tm_p)
        new_p.append((dk32.reshape(bp, seq, D_HEADS, 2 * D_QK), dv32.reshape(bp, seq, D_HEADS, D_V),
                      fk32.reshape(bp, seq, F_HEADS, F_DIM), fv32.reshape(bp, seq, F_HEADS, F_DIM),
                      gates[8:12].T.reshape(bp, seq, F_HEADS),
                      um.reshape(bp, seq, M_WIDTH)[:, seq - (CONV_W - 1):],
                      c_fin, n_fin, m_fin[:, 0, :M_HEADS]))

        outs_s = _in_proj(xs, g_pre, w_main, w_gate_t, gate_bias, tri_s, ns, 1)
        (um, vm, dq, dk32, _, dv32, _, fq, fk32, _, fv32, _, z, gates) = [
            a[:, :nb] if k == 13 else a[:nb] for k, a in enumerate(outs_s)]
        z = outs_s[12]
        pad_tok = lambda a: jnp.pad(a[:, None, :], ((0, 0), (0, L - 1), (0, 0))).reshape(nb * L, -1)
        noop = jnp.where(jnp.arange(GATE_ROWS) < M_HEADS, NEG_INF, 0.0).astype(F32)
        gates_pad = jnp.concatenate(
            [gates[:, :, None], jnp.broadcast_to(noop[:, None, None], (GATE_ROWS, nb, L - 1))],
            axis=2).reshape(GATE_ROWS, nb * L)
        hist = state_mconv[l].astype(F32)
        hist8 = jnp.pad(hist, ((0, 0), (8 - (CONV_W - 1), 0), (0, 0)))
        m0 = jnp.pad(state_mm[l].astype(F32), ((0, 0), (0, LANES - M_HEADS)))[:, None, :]
        hm, c_fin, n_fin, m_fin = _mlstm(pad_tok(um), pad_tok(vm), gates_pad, hist8,
                                         state_mC[l].astype(F32), state_mn[l].astype(F32), m0,
                                         cw, cb, wq_bd, wk_bd, mnorm, mskip, nb, 1)
        hm = hm.reshape(nb, L, M_WIDTH)[:, 0]
        flog_new = gates[8:12].T
        g_new = jnp.pad(flog_new, ((0, 0), (0, 8 - F_HEADS)))[:, :, None]
        od, of = _decode_attn(page_table, l, dq[:, None, :], fq[:, None, :], dk32[:, None, :],
                              dv32[:, None, :], fk32[:, None, :], fv32[:, None, :], g_new,
                              dn, *lams, c_dk, c_dv, c_fk, c_fv, c_flt, pages_per_step, lam_init)
        pad_rows = lambda a: jnp.pad(a, ((0, ns - nb), (0, 0)))
        xs = _out_proj(xs, pad_rows(hm), pad_rows(od[:, 0]), pad_rows(of[:, 0]), z, w_out_l, g_post, ns)
        new_s.append((dk32.reshape(nb, 1, D_HEADS, 2 * D_QK), dv32.reshape(nb, 1, D_HEADS, D_V),
                      fk32.reshape(nb, 1, F_HEADS, F_DIM), fv32.reshape(nb, 1, F_HEADS, F_DIM),
                      flog_new.reshape(nb, 1, F_HEADS),
                      jnp.concatenate([hist[:, 1:], um[:, None, :]], axis=1),
                      c_fin, n_fin, m_fin[:, 0, :M_HEADS]))

    stack = lambda states, k: jnp.stack([st[k] for st in states])
    return ((xp.reshape(bp, seq, D_MODEL), xs[:nb].reshape(nb, 1, D_MODEL))
            + tuple(stack(new_p, k) for k in range(9))
            + tuple(stack(new_s, k) for k in range(9)))
```

```python
import functools
import math

import jax
import jax.numpy as jnp
from jax import lax
from jax.experimental import pallas as pl
from jax.experimental.pallas import tpu as pltpu

D_MODEL = 1024
DEPTH = 4
PAGE_SIZE = 128
M_HEADS = 4
M_DIM = 64
M_WIDTH = M_HEADS * M_DIM
CONV_W = 4
M_CHUNK = 128
D_HEADS = 4
D_QK = 64
D_V = 2 * D_QK
D_WIDTH = D_HEADS * D_V
F_HEADS = 4
F_DIM = 64
F_WIDTH = F_HEADS * F_DIM
MIX_WIDTH = M_WIDTH + D_WIDTH + F_WIDTH
EPS = 1e-6
SPLITS = (M_WIDTH, M_WIDTH, M_HEADS, M_HEADS, D_WIDTH, D_WIDTH, D_WIDTH,
          F_WIDTH, F_WIDTH, F_WIDTH, F_HEADS, MIX_WIDTH)

LANES = 128
GATE_ROWS = 16
VMEM_LIMIT = 56 * 1024 * 1024

F32 = jnp.float32
BF16 = jnp.bfloat16
NEG_INF = float("-inf")

_MAIN = (("um", M_WIDTH), ("vm", M_WIDTH), ("dq", D_WIDTH), ("dk", D_WIDTH), ("dv", D_WIDTH),
         ("fq", F_WIDTH), ("fk", F_WIDTH), ("fv", F_WIDTH), ("z", MIX_WIDTH))
_MAIN_OFF = {}
_o = 0
for _n, _w in _MAIN:
    _MAIN_OFF[_n] = (_o, _w)
    _o += _w
MAIN_WIDTH = _o


def _cparams(sem):
    return pltpu.CompilerParams(dimension_semantics=sem, vmem_limit_bytes=VMEM_LIMIT)


def _log_sigmoid(x):
    return jnp.minimum(x, 0.0) - jnp.log1p(jnp.exp(-jnp.abs(x)))


def _silu(x):
    return x * (1.0 / (1.0 + jnp.exp(-x)))


def _dot(a, b):
    return jnp.dot(a, b, preferred_element_type=F32)


def _dot_nt(a, b):
    return lax.dot_general(a, b, (((1,), (1,)), ((), ())), preferred_element_type=F32)


def _dot_tn(a, b):
    return lax.dot_general(a, b, (((0,), (0,)), ((), ())), preferred_element_type=F32)


def _split3(x):
    hi = x.astype(BF16)
    r1 = x - hi.astype(F32)
    mid = r1.astype(BF16)
    lo = (r1 - mid.astype(F32)).astype(BF16)
    return hi, mid, lo


def _in_proj_kernel(x_ref, g_ref, w_ref, wg_ref, gb_ref, tri_ref,
                    um_ref, vm_ref, dq_ref, dk32_ref, dk16_ref, dv32_ref, dv16_ref,
                    fq_ref, fk32_ref, fk16_ref, fv32_ref, fv16_ref, z_ref, gates_ref,
                    carry_ref, *, tiles_per_seq):
    i = pl.program_id(0)
    x = x_ref[...]
    xn = x * lax.rsqrt(jnp.mean(x * x, axis=-1, keepdims=True) + EPS) * g_ref[...]
    xb = xn.astype(BF16)

    def proj(name):
        off, width = _MAIN_OFF[name]
        return _dot(xb, w_ref[:, off:off + width])

    um_ref[...] = proj("um")
    vm_ref[...] = proj("vm")
    dq_ref[...] = proj("dq").astype(BF16)
    dk = proj("dk")
    dk32_ref[...] = dk
    dk16_ref[...] = dk.astype(BF16)
    dv = proj("dv")
    dv32_ref[...] = dv
    dv16_ref[...] = dv.astype(BF16)
    fq_ref[...] = proj("fq").astype(BF16)
    fk = proj("fk")
    fk32_ref[...] = fk
    fk16_ref[...] = fk.astype(BF16)
    fv = proj("fv")
    fv32_ref[...] = fv
    fv16_ref[...] = fv.astype(BF16)
    z_ref[...] = proj("z")

    pre = _dot_nt(wg_ref[...], xb) + gb_ref[...]
    row = lax.broadcasted_iota(jnp.int32, pre.shape, 0)
    act = jnp.where(row < M_HEADS, pre, _log_sigmoid(pre))

    @pl.when(i % tiles_per_seq == 0)
    def _():
        carry_ref[...] = jnp.zeros_like(carry_ref)

    hi, mid, lo = _split3(act)
    tri = tri_ref[...]
    cs = _dot(hi, tri) + _dot(mid, tri) + _dot(lo, tri) + carry_ref[:, 0:1]
    carry_ref[...] = jnp.broadcast_to(cs[:, cs.shape[1] - 1:], carry_ref.shape)
    shifted = pltpu.roll(cs, 4, axis=0)
    gates_ref[...] = jnp.where((row >= 12), shifted, jnp.where(row < 12, act, 0.0))


def _in_proj(x2d, norm_g, w_main, w_gate_t, gate_bias, tri, tm, tiles_per_seq):
    n = x2d.shape[0]
    grid = (n // tm,)
    row_spec = lambda w: pl.BlockSpec((tm, w), lambda i: (i, 0))
    const = lambda shape: pl.BlockSpec(shape, lambda i: (0,) * len(shape))
    out_shape = [
        jax.ShapeDtypeStruct((n, M_WIDTH), F32),
        jax.ShapeDtypeStruct((n, M_WIDTH), F32),
        jax.ShapeDtypeStruct((n, D_WIDTH), BF16),
        jax.ShapeDtypeStruct((n, D_WIDTH), F32),
        jax.ShapeDtypeStruct((n, D_WIDTH), BF16),
        jax.ShapeDtypeStruct((n, D_WIDTH), F32),
        jax.ShapeDtypeStruct((n, D_WIDTH), BF16),
        jax.ShapeDtypeStruct((n, F_WIDTH), BF16),
        jax.ShapeDtypeStruct((n, F_WIDTH), F32),
        jax.ShapeDtypeStruct((n, F_WIDTH), BF16),
        jax.ShapeDtypeStruct((n, F_WIDTH), F32),
        jax.ShapeDtypeStruct((n, F_WIDTH), BF16),
        jax.ShapeDtypeStruct((n, MIX_WIDTH), F32),
        jax.ShapeDtypeStruct((GATE_ROWS, n), F32),
    ]
    out_specs = [row_spec(s.shape[1]) for s in out_shape[:-1]]
    out_specs.append(pl.BlockSpec((GATE_ROWS, tm), lambda i: (0, i)))
    return pl.pallas_call(
        functools.partial(_in_proj_kernel, tiles_per_seq=tiles_per_seq),
        grid=grid,
        in_specs=[row_spec(D_MODEL), const((1, D_MODEL)), const((D_MODEL, MAIN_WIDTH)),
                  const((GATE_ROWS, D_MODEL)), const((GATE_ROWS, 1)), const((tm, tm))],
        out_specs=out_specs,
        out_shape=out_shape,
        scratch_shapes=[pltpu.VMEM((GATE_ROWS, LANES), F32)],
        compiler_params=_cparams(("arbitrary",)),
        name="in_proj",
    )(x2d, norm_g, w_main, w_gate_t, gate_bias, tri)


def _mlstm_kernel(um_ref, vm_ref, gates_ref, hist_ref, c0_ref, n0_ref, m0_ref,
                  cw_ref, cb_ref, wq_ref, wk_ref, mnorm_ref, mskip_ref,
                  hm_ref, c_out_ref, n_out_ref, m_out_ref,
                  ubuf, c_s, n_s, m_s):
    c = pl.program_id(1)
    L = M_CHUNK

    @pl.when(c == 0)
    def _():
        ubuf[0:8, :] = hist_ref[0]
        c_s[...] = c0_ref[0]
        n_s[...] = n0_ref[0]
        m_s[...] = m0_ref[0]

    ubuf[8:8 + L, :] = um_ref[...]
    conv = cb_ref[...]
    for j in range(CONV_W):
        conv = conv + ubuf[5 + j:5 + j + L, :] * cw_ref[j:j + 1, :]
    ubuf[0:8, :] = ubuf[L:L + 8, :]
    xc = _silu(conv)
    xb = xc.astype(BF16)
    q_all = _dot(xb, wq_ref[...])
    k_all = _dot(xb, wk_ref[...])
    v_all = vm_ref[...]

    gates = gates_ref[...]
    lane = lax.broadcasted_iota(jnp.int32, gates.shape, 1)
    bcum = gates
    shift = 1
    while shift < L:
        bcum = bcum + jnp.where(lane >= shift, pltpu.roll(bcum, shift, axis=1), 0.0)
        shift *= 2

    ri = lax.broadcasted_iota(jnp.int32, (L, L), 0)
    ci = lax.broadcasted_iota(jnp.int32, (L, L), 1)
    causal = ci <= ri
    eye = ci == ri

    outs = []
    m_prev_all = m_s[...]
    m_new_all = m_prev_all
    lane1 = lax.broadcasted_iota(jnp.int32, m_prev_all.shape, 1)
    for h in range(M_HEADS):
        sl = slice(h * M_DIM, (h + 1) * M_DIM)
        q_h, k_h, v_h = q_all[:, sl], k_all[:, sl], v_all[:, sl]
        li_row = gates[h:h + 1, :]
        lf_row = gates[M_HEADS + h:M_HEADS + h + 1, :]
        a_row = li_row - bcum[M_HEADS + h:M_HEADS + h + 1, :]
        m_prev = m_prev_all[:, h:h + 1]
        c_prev = c_s[h]
        n_prev = n_s[h:h + 1, :]

        a_mat = jnp.where(causal, a_row, NEG_INF)
        g = jnp.maximum(jnp.max(a_mat, axis=1, keepdims=True), m_prev)
        b_col = jnp.sum(jnp.where(causal, lf_row, 0.0), axis=1, keepdims=True)
        w = jnp.exp(a_mat - g)
        w_inter = jnp.exp(m_prev - g)
        qb = q_h.astype(BF16)
        s = _dot_nt(qb, k_h.astype(BF16)) * w
        num = w_inter * _dot(qb, c_prev.astype(BF16)) + _dot(s.astype(BF16), v_h.astype(BF16))
        den = (w_inter * jnp.sum(q_h * n_prev, axis=1, keepdims=True)
               + jnp.sum(s, axis=1, keepdims=True))
        m_t = b_col + g
        hh = num / jnp.maximum(jnp.abs(den), jnp.exp(-m_t))
        hn = hh * lax.rsqrt(jnp.mean(hh * hh, axis=1, keepdims=True) + EPS)
        outs.append(hn * mnorm_ref[:, sl] + mskip_ref[:, sl] * xc[:, sl])

        g_last = g[L - 1:L, :]
        decay = jnp.exp(m_prev - g_last)
        w_last_row = w[L - 1:L, :]
        w_last_col = jnp.sum(jnp.where(eye, w_last_row, 0.0), axis=1, keepdims=True)
        kw = k_h * w_last_col
        c_s[h] = decay * c_prev + _dot_tn(kw.astype(BF16), v_h.astype(BF16))
        n_s[h:h + 1, :] = decay * n_prev + jnp.sum(kw, axis=0, keepdims=True)
        m_new_all = jnp.where(lane1 == h, m_t[L - 1:L, :], m_new_all)
    m_s[...] = m_new_all
    hm_ref[...] = jnp.concatenate(outs, axis=1)

    @pl.when(c == pl.num_programs(1) - 1)
    def _():
        c_out_ref[0] = c_s[...]
        n_out_ref[0] = n_s[...]
        m_out_ref[0] = m_s[...]


def _mlstm(um, vm, gates, hist8, c0, n0, m0, cw, cb, wq_bd, wk_bd, mnorm, mskip, n_seq, n_chunks):
    n = um.shape[0]
    L = M_CHUNK
    tok = pl.BlockSpec((L, M_WIDTH), lambda b, c: (b * n_chunks + c, 0))
    const = lambda shape: pl.BlockSpec(shape, lambda b, c: (0,) * len(shape))
    per_seq = lambda shape: pl.BlockSpec((1,) + shape, lambda b, c: (b,) + (0,) * len(shape))
    return pl.pallas_call(
        _mlstm_kernel,
        grid=(n_seq, n_chunks),
        in_specs=[tok, tok,
                  pl.BlockSpec((GATE_ROWS, L), lambda b, c: (0, b * n_chunks + c)),
                  per_seq((8, M_WIDTH)), per_seq((M_HEADS, M_DIM, M_DIM)),
                  per_seq((M_HEADS, M_DIM)), per_seq((1, LANES)),
                  const((CONV_W, M_WIDTH)), const((1, M_WIDTH)),
                  const((M_WIDTH, M_WIDTH)), const((M_WIDTH, M_WIDTH)),
                  const((1, M_WIDTH)), const((1, M_WIDTH))],
        out_specs=[tok, per_seq((M_HEADS, M_DIM, M_DIM)), per_seq((M_HEADS, M_DIM)),
                   per_seq((1, LANES))],
        out_shape=[jax.ShapeDtypeStruct((n, M_WIDTH), F32),
                   jax.ShapeDtypeStruct((n_seq, M_HEADS, M_DIM, M_DIM), F32),
                   jax.ShapeDtypeStruct((n_seq, M_HEADS, M_DIM), F32),
                   jax.ShapeDtypeStruct((n_seq, 1, LANES), F32)],
        scratch_shapes=[pltpu.VMEM((L + 8, M_WIDTH), F32),
                        pltpu.VMEM((M_HEADS, M_DIM, M_DIM), F32),
                        pltpu.VMEM((M_HEADS, M_DIM), F32),
                        pltpu.VMEM((1, LANES), F32)],
        compiler_params=_cparams(("arbitrary", "arbitrary")),
        name="mlstm",
    )(um, vm, gates, hist8, c0, n0, m0, cw, cb, wq_bd, wk_bd, mnorm, mskip)


def _lambda(lq1_ref, lk1_ref, lq2_ref, lk2_ref, lam_init):
    e1 = jnp.exp(jnp.sum(lq1_ref[...] * lk1_ref[...], axis=1, keepdims=True))
    e2 = jnp.exp(jnp.sum(lq2_ref[...] * lk2_ref[...], axis=1, keepdims=True))
    return e1 - e2 + lam_init


def _attn_kernel(*refs, tq, tk, rc, fox, lam_init):
    if fox:
        q_ref, k_ref, v_ref, f_ref, o_ref, qs, s_a, s_b, p_a, p_b, al_a, al_b, m_s, l_s, acc = refs
    else:
        (q_ref, k_ref, v_ref, dn_ref, lq1_ref, lk1_ref, lq2_ref, lk2_ref,
         o_ref, qs, s_a, s_b, p_a, p_b, al_a, al_b, m_s, l_s, acc) = refs
    s_buf, p_buf, al_buf = (s_a, s_b), (p_a, p_b), (al_a, al_b)
    i = pl.program_id(2)
    j_last = ((i + 1) * tq - 1) // tk
    half = LANES // 2
    f_row = 12 + 2 * pl.program_id(1)

    q = q_ref[...].astype(F32)
    lane = lax.broadcasted_iota(jnp.int32, q.shape, 1)
    qs[0:tq, :] = jnp.where(lane < half, q, 0.0).astype(BF16)
    qs[tq:2 * tq, :] = jnp.where(lane >= half, q, 0.0).astype(BF16)
    m_s[...] = jnp.full_like(m_s, NEG_INF)
    l_s[...] = jnp.zeros_like(l_s)
    acc[...] = jnp.zeros_like(acc)
    p_b[...] = jnp.zeros_like(p_b)
    al_b[...] = jnp.ones_like(al_b)

    diag = (lax.broadcasted_iota(jnp.int32, (rc, tk), 1) - lax.broadcasted_iota(jnp.int32, (rc, tk), 0))

    def scores(t, slot):
        k0 = pl.multiple_of(t * tk, tk)
        s_buf[slot][...] = _dot_nt(qs[...], k_ref[pl.ds(k0, tk), :])

    def values(t, slot):
        k0 = pl.multiple_of(jnp.maximum(t, 0) * tk, tk)
        acc[...] = al_buf[slot][...] * acc[...] + _dot(p_buf[slot][...], v_ref[pl.ds(k0, tk), :])

    def softmax(t, slot, masked):
        k0 = pl.multiple_of(t * tk, tk)
        if fox:
            f_top = f_ref[pl.ds(f_row, 1), pl.ds(k0, tk)]
            f_bot = f_ref[pl.ds(f_row + 1, 1), pl.ds(k0, tk)]
        for c in range(2 * tq // rc):
            rows = slice(c * rc, (c + 1) * rc)
            s = s_buf[slot][rows, :]
            if fox:
                s = s - (f_top if c * rc < tq else f_bot)
            if masked:
                s = jnp.where(diag <= i * tq + (c * rc) % tq - k0, s, NEG_INF)
            m_prev = m_s[rows, :]
            m_new = jnp.maximum(m_prev, jnp.max(s, axis=1, keepdims=True))
            alpha = jnp.exp(m_prev - m_new)
            p = jnp.exp(s - pltpu.repeat(m_new, tk // LANES, axis=1))
            l_s[rows, :] = alpha * l_s[rows, :] + jnp.sum(p, axis=1, keepdims=True)
            m_s[rows, :] = m_new
            al_buf[slot][rows, :] = alpha
            p_buf[slot][rows, :] = p.astype(BF16)

    def step(t, slot):
        scores(t + 1, 1 - slot)
        values(t - 1, 1 - slot)
        softmax(t, slot, False)

    def last(slot):
        values(j_last - 1, 1 - slot)
        softmax(j_last, slot, True)
        values(j_last, slot)

    scores(0, 0)

    def pair(u, carry):
        step(2 * u, 0)
        step(2 * u + 1, 1)
        return carry

    lax.fori_loop(0, j_last // 2, pair, 0)

    @pl.when(j_last % 2 == 1)
    def _():
        step(j_last - 1, 0)
        last(1)

    @pl.when(j_last % 2 == 0)
    def _():
        last(0)

    o = acc[...] / l_s[...]
    top, bot = o[0:tq, :], o[tq:2 * tq, :]
    if fox:
        o_ref[...] = jnp.where(lane < half, top, bot)
    else:
        lam = _lambda(lq1_ref, lk1_ref, lq2_ref, lk2_ref, lam_init)
        od = top - lam * bot
        od = od * lax.rsqrt(jnp.mean(od * od, axis=1, keepdims=True) + EPS)
        o_ref[...] = od * dn_ref[...] * (1.0 - lam_init)


def _prompt_attn(q, k, v, extra, n_seq, seq, tq, tk, rc, fox, lam_init):
    n, width = q.shape
    groups = width // LANES
    nq = seq // tq
    q_spec = pl.BlockSpec((tq, LANES), lambda b, h, i: (b * nq + i, h))
    kv_spec = pl.BlockSpec((seq, LANES), lambda b, h, i: (b, h))
    if fox:
        extra_specs = [pl.BlockSpec((GATE_ROWS, seq), lambda b, h, i: (0, b))]
    else:
        lam_spec = pl.BlockSpec((1, D_QK), lambda b, h, i: (0, 0))
        extra_specs = [pl.BlockSpec((1, LANES), lambda b, h, i: (0, h))] + [lam_spec] * 4
    return pl.pallas_call(
        functools.partial(_attn_kernel, tq=tq, tk=tk, rc=rc, fox=fox, lam_init=lam_init),
        grid=(n_seq, groups, nq),
        in_specs=[q_spec, kv_spec, kv_spec] + extra_specs,
        out_specs=q_spec,
        out_shape=jax.ShapeDtypeStruct((n, width), F32),
        scratch_shapes=[pltpu.VMEM((2 * tq, LANES), BF16)]
                       + [pltpu.VMEM((2 * tq, tk), F32)] * 2
                       + [pltpu.VMEM((2 * tq, tk), BF16)] * 2
                       + [pltpu.VMEM((2 * tq, LANES), F32)] * 2
                       + [pltpu.VMEM((2 * tq, LANES), F32)] * 3,
        compiler_params=_cparams(("arbitrary",) * 3),
        name="fox_attn" if fox else "diff_attn",
    )(q, k, v, *extra)


D_ROWS = PAGE_SIZE * D_HEADS


def _rows_from_segments(row_vec, n_seg, r):
    out = jnp.zeros((8, LANES), F32)
    for s in range(n_seg):
        seg = jnp.broadcast_to(row_vec[:, s * LANES:(s + 1) * LANES], (8, LANES))
        out = jnp.where(lax.shift_right_logical(r, 1) == s, seg, out)
    return out


def _decode_kernel(pt_ref, *refs, pages_per_step, lam_init):
    G = pages_per_step
    (dq_ref, fq_ref, dkn_ref, dvn_ref, fkn_ref, fvn_ref, gn_ref,
     dn_ref, lq1_ref, lk1_ref, lq2_ref, lk2_ref, tri_ref) = refs[:13]
    page_refs = refs[13:13 + 5 * G]
    od_ref, of_ref = refs[13 + 5 * G:15 + 5 * G]
    qd_s, qf_s, md_s, ld_s, accd, mf_s, lf_s, accf, fcar = refs[15 + 5 * G:]
    j = pl.program_id(1)
    r8 = lax.broadcasted_iota(jnp.int32, (8, LANES), 0)
    l8 = lax.broadcasted_iota(jnp.int32, (8, LANES), 1)
    half_sel = lax.shift_right_logical(l8, 6) == (r8 & 1)

    def head_cols(shape):
        r = lax.broadcasted_iota(jnp.int32, shape, 0)
        c = lax.broadcasted_iota(jnp.int32, shape, 1)
        return lax.shift_right_logical(c, 6) == r

    @pl.when(j == 0)
    def _():
        qd = _rows_from_segments(dq_ref[0].astype(F32), D_HEADS, r8)
        qd_s[...] = jnp.where(half_sel, qd, 0.0).astype(BF16)
        qf = jnp.broadcast_to(fq_ref[0].astype(F32), (8, F_WIDTH))
        qf_s[...] = jnp.where(head_cols(qf.shape), qf, 0.0).astype(BF16)
        md_s[...] = jnp.full_like(md_s, NEG_INF)
        mf_s[...] = jnp.full_like(mf_s, NEG_INF)
        ld_s[...] = jnp.zeros_like(ld_s)
        lf_s[...] = jnp.zeros_like(lf_s)
        accd[...] = jnp.zeros_like(accd)
        accf[...] = jnp.zeros_like(accf)
        fcar[...] = jnp.zeros_like(fcar)

    def online(s, v_refs, width, pv_dot, m_ref, l_ref, acc_ref):
        m_prev = m_ref[...]
        m_new = jnp.maximum(m_prev, jnp.max(s, axis=1, keepdims=True))
        alpha = jnp.exp(m_prev - m_new)
        p = jnp.exp(s - m_new)
        l_ref[...] = alpha * l_ref[...] + jnp.sum(p, axis=1, keepdims=True)
        pb = p.astype(BF16)
        pv = pv_dot(pb[:, 0:width], v_refs[0][0, 0].astype(BF16))
        for g in range(1, G):
            pv = pv + pv_dot(pb[:, g * width:(g + 1) * width], v_refs[g][0, 0].astype(BF16))
        acc_ref[...] = alpha * acc_ref[...] + pv
        m_ref[...] = m_new

    kd_refs = page_refs[0::5]
    vd_refs = page_refs[1::5]
    kf_refs = page_refs[2::5]
    vf_refs = page_refs[3::5]
    fl_refs = page_refs[4::5]

    qd = qd_s[...]
    s_d = jnp.concatenate([_dot_nt(qd, kd_refs[g][0, 0].astype(BF16)) for g in range(G)], axis=1)
    rd = lax.broadcasted_iota(jnp.int32, s_d.shape, 0)
    cd = lax.broadcasted_iota(jnp.int32, s_d.shape, 1)
    s_d = jnp.where((cd & (D_HEADS - 1)) == lax.shift_right_logical(rd, 1), s_d, NEG_INF)
    online(s_d, vd_refs, D_ROWS, _dot, md_s, ld_s, accd)

    qf = qf_s[...]
    tri = tri_ref[...]
    carry = fcar[...]
    parts = []
    for g in range(G):
        fl = fl_refs[g][0, 0]
        hi, mid, lo = _split3(fl)
        cum = _dot(hi, tri) + _dot(mid, tri) + _dot(lo, tri) + carry
        carry = carry + jnp.sum(fl, axis=1, keepdims=True)
        parts.append(_dot(qf, kf_refs[g][0, 0].astype(BF16)) - cum)
    fcar[...] = carry
    online(jnp.concatenate(parts, axis=1), vf_refs, PAGE_SIZE, _dot_nt, mf_s, lf_s, accf)

    @pl.when(j == pl.num_programs(1) - 1)
    def _():
        def add_new(q_s, k_new, v_new, bias, m_ref, l_ref, acc_ref):
            s = jnp.sum(q_s[...].astype(F32) * k_new, axis=1, keepdims=True) - bias
            m_prev = m_ref[...]
            m_new = jnp.maximum(m_prev, s)
            alpha = jnp.exp(m_prev - m_new)
            p = jnp.exp(s - m_new)
            l_new = alpha * l_ref[...] + p
            return (alpha * acc_ref[...] + p * v_new) / l_new

        od_all = add_new(qd_s, _rows_from_segments(dkn_ref[0], D_HEADS, r8),
                         _rows_from_segments(dvn_ref[0], D_HEADS, r8), 0.0, md_s, ld_s, accd)
        f_new = fcar[...] + gn_ref[0]
        of_all = add_new(qf_s, fkn_ref[0], fvn_ref[0], f_new, mf_s, lf_s, accf)

        lam = _lambda(lq1_ref, lk1_ref, lq2_ref, lk2_ref, lam_init)
        parts_d = []
        for h in range(D_HEADS):
            seg = od_all[2 * h:2 * h + 1, :] - lam * od_all[2 * h + 1:2 * h + 2, :]
            parts_d.append(seg * lax.rsqrt(jnp.mean(seg * seg, axis=1, keepdims=True) + EPS))
        od_ref[0] = jnp.concatenate(parts_d, axis=1) * dn_ref[...] * (1.0 - lam_init)
        of_ref[0] = jnp.sum(jnp.where(head_cols(of_all.shape), of_all, 0.0), axis=0, keepdims=True)


def _decode_attn(page_table, layer, dq, fq, dk_new, dv_new, fk_new, fv_new, g_new,
                 d_norm, lq1, lk1, lq2, lk2, tri, c_dk, c_dv, c_fk, c_fv, c_flt,
                 pages_per_step, lam_init):
    nb, n_pages = page_table.shape
    G = pages_per_step
    steps = n_pages // G
    per_b = lambda w: pl.BlockSpec((1, 1, w), lambda b, j, pt: (b, 0, 0))
    const = lambda shape: pl.BlockSpec(shape, lambda b, j, pt: (0,) * len(shape))
    in_specs = [per_b(D_WIDTH), per_b(F_WIDTH), per_b(D_WIDTH), per_b(D_WIDTH),
                per_b(F_WIDTH), per_b(F_WIDTH),
                pl.BlockSpec((1, 8, 1), lambda b, j, pt: (b, 0, 0)),
                const((1, D_WIDTH))] + [const((1, D_QK))] * 4 + [const((PAGE_SIZE, PAGE_SIZE))]
    args = [dq, fq, dk_new, dv_new, fk_new, fv_new, g_new, d_norm, lq1, lk1, lq2, lk2, tri]
    for g in range(G):
        def page(rows, w, g=g):
            return pl.BlockSpec((1, 1, rows, w),
                                lambda b, j, pt: (layer, pt[b, j * G + g], 0, 0))
        in_specs += [page(D_ROWS, LANES), page(D_ROWS, LANES),
                     page(F_WIDTH, PAGE_SIZE), page(F_WIDTH, PAGE_SIZE), page(8, PAGE_SIZE)]
        args += [c_dk, c_dv, c_fk, c_fv, c_flt]
    grid_spec = pltpu.PrefetchScalarGridSpec(
        num_scalar_prefetch=1,
        grid=(nb, steps),
        in_specs=in_specs,
        out_specs=[per_b(D_WIDTH), per_b(F_WIDTH)],
        scratch_shapes=[pltpu.VMEM((8, LANES), BF16), pltpu.VMEM((8, F_WIDTH), BF16),
                        pltpu.VMEM((8, 1), F32), pltpu.VMEM((8, 1), F32),
                        pltpu.VMEM((8, LANES), F32),
                        pltpu.VMEM((8, 1), F32), pltpu.VMEM((8, 1), F32),
                        pltpu.VMEM((8, F_WIDTH), F32),
                        pltpu.VMEM((8, 1), F32)])
    return pl.pallas_call(
        functools.partial(_decode_kernel, pages_per_step=G, lam_init=lam_init),
        grid_spec=grid_spec,
        out_shape=[jax.ShapeDtypeStruct((nb, 1, D_WIDTH), F32),
                   jax.ShapeDtypeStruct((nb, 1, F_WIDTH), F32)],
        compiler_params=_cparams(("arbitrary", "arbitrary")),
        name="decode_attn",
    )(page_table, *args)


def _out_proj_kernel(x_ref, hm_ref, od_ref, of_ref, z_ref, w_ref, g_ref, o_ref):
    h = jnp.concatenate([hm_ref[...], od_ref[...], of_ref[...]], axis=1) * _silu(z_ref[...])
    y = _dot(h.astype(BF16), w_ref[...])
    yn = y * lax.rsqrt(jnp.mean(y * y, axis=-1, keepdims=True) + EPS) * g_ref[...]
    o_ref[...] = x_ref[...] + yn


def _out_proj(x2d, hm, od, of, z, w_out, norm_g, tm):
    n = x2d.shape[0]
    row_spec = lambda w: pl.BlockSpec((tm, w), lambda i: (i, 0))
    const = lambda shape: pl.BlockSpec(shape, lambda i: (0,) * len(shape))
    return pl.pallas_call(
        _out_proj_kernel,
        grid=(n // tm,),
        in_specs=[row_spec(D_MODEL), row_spec(M_WIDTH), row_spec(D_WIDTH), row_spec(F_WIDTH),
                  row_spec(MIX_WIDTH), const((MIX_WIDTH, D_MODEL)), const((1, D_MODEL))],
        out_specs=row_spec(D_MODEL),
        out_shape=jax.ShapeDtypeStruct((n, D_MODEL), F32),
        compiler_params=_cparams(("arbitrary",)),
        name="out_proj",
    )(x2d, hm, od, of, z, w_out, norm_g)


def _block_diag(w):
    hh, d, _ = w.shape
    out = jnp.zeros((hh * d, hh * d), w.dtype)
    for h in range(hh):
        out = out.at[h * d:(h + 1) * d, h * d:(h + 1) * d].set(w[h])
    return out


def _layer_weights(l, w_in, b_i, b_f, b_fox, wq_m, wk_m):
    idx = [0]
    for s in SPLITS:
        idx.append(idx[-1] + s)
    cols = lambda k: w_in[l][:, idx[k]:idx[k + 1]]
    u_m, v_m, ig, fg, dq, dk, dv, fq, fk, fv, ffg, z = [cols(k) for k in range(len(SPLITS))]
    w_main = jnp.concatenate([u_m, v_m, dq * (D_QK ** -0.5), dk, dv, fq * (F_DIM ** -0.5), fk, fv, z],
                             axis=1).astype(BF16)
    w_gate_t = jnp.concatenate([ig, fg, ffg, jnp.zeros((D_MODEL, GATE_ROWS - 12), F32)],
                               axis=1).T.astype(BF16)
    gate_bias = jnp.concatenate([b_i[l], b_f[l], b_fox[l], jnp.zeros((GATE_ROWS - 12,), F32)]
                                ).astype(F32).reshape(GATE_ROWS, 1)
    wq_bd = _block_diag(wq_m[l]).astype(BF16)
    wk_bd = (_block_diag(wk_m[l]) * (M_DIM ** -0.5)).astype(BF16)
    return w_main, w_gate_t, gate_bias, wq_bd, wk_bd


def kernel(x_prompt, x_sample, cache_dk, cache_dv, cache_fk, cache_fv, cache_flogf, state_mconv, state_mC, state_mn, state_mm, page_table, norm_pre, norm_post, w_in, conv_w, conv_b, wq_m, wk_m, b_i, b_f, m_norm, m_skip, lam_q1, lam_k1, lam_q2, lam_k2, d_norm, b_fox, w_out):
    bp, seq, _ = x_prompt.shape
    nb = x_sample.shape[0]
    n_pool = cache_dk.shape[1]
    n_p = bp * seq
    tm_p = 512
    tq, tk = 256, 512
    rc = 64
    pages_per_step = 8
    L = M_CHUNK

    xp = x_prompt.reshape(n_p, D_MODEL)
    ns = LANES
    xs = jnp.pad(x_sample.reshape(nb, D_MODEL), ((0, ns - nb), (0, 0)))
    c_dk = cache_dk.reshape(DEPTH, n_pool, D_ROWS, LANES)
    c_dv = cache_dv.reshape(DEPTH, n_pool, D_ROWS, LANES)
    c_fk = jnp.transpose(cache_fk, (0, 1, 3, 4, 2)).reshape(DEPTH, n_pool, F_WIDTH, PAGE_SIZE)
    c_fv = jnp.transpose(cache_fv, (0, 1, 3, 4, 2)).reshape(DEPTH, n_pool, F_WIDTH, PAGE_SIZE)
    c_flt = jnp.pad(jnp.swapaxes(cache_flogf.astype(F32), 2, 3), ((0, 0), (0, 0), (0, 8 - F_HEADS), (0, 0)))

    tri_p = (lax.broadcasted_iota(jnp.int32, (tm_p, tm_p), 0)
             <= lax.broadcasted_iota(jnp.int32, (tm_p, tm_p), 1)).astype(BF16)
    tri_s = jnp.eye(ns, dtype=BF16)
    tri_page = tri_p[:PAGE_SIZE, :PAGE_SIZE]

    zero_hist = jnp.zeros((bp, 8, M_WIDTH), F32)
    zero_c = jnp.zeros((bp, M_HEADS, M_DIM, M_DIM), F32)
    zero_n = jnp.zeros((bp, M_HEADS, M_DIM), F32)
    zero_m = jnp.zeros((bp, 1, LANES), F32)

    new_p, new_s = [], []
    for l in range(DEPTH):
        lam_init = 0.8 - 0.6 * math.exp(-0.3 * l)
        w_main, w_gate_t, gate_bias, wq_bd, wk_bd = _layer_weights(l, w_in, b_i, b_f, b_fox, wq_m, wk_m)
        w_out_l = w_out[l].astype(BF16)
        g_pre = norm_pre[l].reshape(1, D_MODEL)
        g_post = norm_post[l].reshape(1, D_MODEL)
        cw, cb = conv_w[l], conv_b[l].reshape(1, M_WIDTH)
        mnorm, mskip = m_norm[l].reshape(1, M_WIDTH), m_skip[l].reshape(1, M_WIDTH)
        dn = d_norm[l].reshape(1, D_WIDTH)
        lams = [a[l].reshape(1, D_QK).astype(F32) for a in (lam_q1, lam_k1, lam_q2, lam_k2)]

        (um, vm, dq, dk32, dk16, dv32, dv16, fq, fk32, fk16, fv32, fv16, z, gates) = _in_proj(
            xp, g_pre, w_main, w_gate_t, gate_bias, tri_p, tm_p, seq // tm_p)
        hm, c_fin, n_fin, m_fin = _mlstm(um, vm, gates, zero_hist, zero_c, zero_n, zero_m,
                                         cw, cb, wq_bd, wk_bd, mnorm, mskip, bp, seq // L)
        od = _prompt_attn(dq, dk16, dv16, [dn] + lams, bp, seq, tq, tk, rc, False, lam_init)
        of = _prompt_attn(fq, fk16, fv16, [gates], bp, seq, tq, tk, rc, True, lam_init)
        xp = _out_proj(xp, hm, od, of, z, w_out_l, g_post, tm_p)
        new_p.append((dk32.reshape(bp, seq, D_HEADS, 2 * D_QK), dv32.reshape(bp, seq, D_HEADS, D_V),
                      fk32.reshape(bp, seq, F_HEADS, F_DIM), fv32.reshape(bp, seq, F_HEADS, F_DIM),
                      gates[8:12].T.reshape(bp, seq, F_HEADS),
                      um.reshape(bp, seq, M_WIDTH)[:, seq - (CONV_W - 1):],
                      c_fin, n_fin, m_fin[:, 0, :M_HEADS]))

        outs_s = _in_proj(xs, g_pre, w_main, w_gate_t, gate_bias, tri_s, ns, 1)
        (um, vm, dq, dk32, _, dv32, _, fq, fk32, _, fv32, _, z, gates) = [
            a[:, :nb] if k == 13 else a[:nb] for k, a in enumerate(outs_s)]
        z = outs_s[12]
        pad_tok = lambda a: jnp.pad(a[:, None, :], ((0, 0), (0, L - 1), (0, 0))).reshape(nb * L, -1)
        noop = jnp.where(jnp.arange(GATE_ROWS) < M_HEADS, NEG_INF, 0.0).astype(F32)
        gates_pad = jnp.concatenate(
            [gates[:, :, None], jnp.broadcast_to(noop[:, None, None], (GATE_ROWS, nb, L - 1))],
            axis=2).reshape(GATE_ROWS, nb * L)
        hist = state_mconv[l].astype(F32)
        hist8 = jnp.pad(hist, ((0, 0), (8 - (CONV_W - 1), 0), (0, 0)))
        m0 = jnp.pad(state_mm[l].astype(F32), ((0, 0), (0, LANES - M_HEADS)))[:, None, :]
        hm, c_fin, n_fin, m_fin = _mlstm(pad_tok(um), pad_tok(vm), gates_pad, hist8,
                                         state_mC[l].astype(F32), state_mn[l].astype(F32), m0,
                                         cw, cb, wq_bd, wk_bd, mnorm, mskip, nb, 1)
        hm = hm.reshape(nb, L, M_WIDTH)[:, 0]
        flog_new = gates[8:12].T
        g_new = jnp.pad(flog_new, ((0, 0), (0, 8 - F_HEADS)))[:, :, None]
        od, of = _decode_attn(page_table, l, dq[:, None, :], fq[:, None, :], dk32[:, None, :],
                              dv32[:, None, :], fk32[:, None, :], fv32[:, None, :], g_new,
                              dn, *lams, tri_page, c_dk, c_dv, c_fk, c_fv, c_flt, pages_per_step, lam_init)
        pad_rows = lambda a: jnp.pad(a, ((0, ns - nb), (0, 0)))
        xs = _out_proj(xs, pad_rows(hm), pad_rows(od[:, 0]), pad_rows(of[:, 0]), z, w_out_l, g_post, ns)
        new_s.append((dk32.reshape(nb, 1, D_HEADS, 2 * D_QK), dv32.reshape(nb, 1, D_HEADS, D_V),
                      fk32.reshape(nb, 1, F_HEADS, F_DIM), fv32.reshape(nb, 1, F_HEADS, F_DIM),
                      flog_new.reshape(nb, 1, F_HEADS),
                      jnp.concatenate([hist[:, 1:], um[:, None, :]], axis=1),
                      c_fin, n_fin, m_fin[:, 0, :M_HEADS]))

    stack = lambda states, k: jnp.stack([st[k] for st in states])
    return ((xp.reshape(bp, seq, D_MODEL), xs[:nb].reshape(nb, 1, D_MODEL))
            + tuple(stack(new_p, k) for k in range(9))
            + tuple(stack(new_s, k) for k in range(9)))
```

```python
import functools
import math

import jax
import jax.numpy as jnp
from jax import lax
from jax.experimental import pallas as pl
from jax.experimental.pallas import tpu as pltpu

D_MODEL = 1024
DEPTH = 4
PAGE_SIZE = 128
M_HEADS = 4
M_DIM = 64
M_WIDTH = M_HEADS * M_DIM
CONV_W = 4
M_CHUNK = 128
D_HEADS = 4
D_QK = 64
D_V = 2 * D_QK
D_WIDTH = D_HEADS * D_V
F_HEADS = 4
F_DIM = 64
F_WIDTH = F_HEADS * F_DIM
MIX_WIDTH = M_WIDTH + D_WIDTH + F_WIDTH
EPS = 1e-6
SPLITS = (M_WIDTH, M_WIDTH, M_HEADS, M_HEADS, D_WIDTH, D_WIDTH, D_WIDTH,
          F_WIDTH, F_WIDTH, F_WIDTH, F_HEADS, MIX_WIDTH)

LANES = 128
GATE_ROWS = 16
VMEM_LIMIT = 56 * 1024 * 1024

F32 = jnp.float32
BF16 = jnp.bfloat16
NEG_INF = float("-inf")

_MAIN = (("um", M_WIDTH), ("vm", M_WIDTH), ("dq", D_WIDTH), ("dk", D_WIDTH), ("dv", D_WIDTH),
         ("fq", F_WIDTH), ("fk", F_WIDTH), ("fv", F_WIDTH), ("z", MIX_WIDTH))
_MAIN_OFF = {}
_o = 0
for _n, _w in _MAIN:
    _MAIN_OFF[_n] = (_o, _w)
    _o += _w
MAIN_WIDTH = _o


def _cparams(sem):
    return pltpu.CompilerParams(dimension_semantics=sem, vmem_limit_bytes=VMEM_LIMIT)


def _log_sigmoid(x):
    return jnp.minimum(x, 0.0) - jnp.log1p(jnp.exp(-jnp.abs(x)))


def _silu(x):
    return x * (1.0 / (1.0 + jnp.exp(-x)))


def _dot(a, b):
    return jnp.dot(a, b, preferred_element_type=F32)


def _dot_nt(a, b):
    return lax.dot_general(a, b, (((1,), (1,)), ((), ())), preferred_element_type=F32)


def _dot_tn(a, b):
    return lax.dot_general(a, b, (((0,), (0,)), ((), ())), preferred_element_type=F32)


def _split3(x):
    hi = x.astype(BF16)
    r1 = x - hi.astype(F32)
    mid = r1.astype(BF16)
    lo = (r1 - mid.astype(F32)).astype(BF16)
    return hi, mid, lo


def _in_proj_kernel(x_ref, g_ref, w_ref, wg_ref, gb_ref, tri_ref,
                    um_ref, vm_ref, dq_ref, dk32_ref, dk16_ref, dv32_ref, dv16_ref,
                    fq_ref, fk32_ref, fk16_ref, fv32_ref, fv16_ref, z_ref, gates_ref,
                    carry_ref, *, tiles_per_seq):
    i = pl.program_id(0)
    x = x_ref[...]
    xn = x * lax.rsqrt(jnp.mean(x * x, axis=-1, keepdims=True) + EPS) * g_ref[...]
    xb = xn.astype(BF16)

    def proj(name):
        off, width = _MAIN_OFF[name]
        return _dot(xb, w_ref[:, off:off + width])

    um_ref[...] = proj("um")
    vm_ref[...] = proj("vm")
    dq_ref[...] = proj("dq").astype(BF16)
    dk = proj("dk")
    tm = x.shape[0]
    for h in range(D_HEADS):
        dk32_ref[pl.ds(h, tm, stride=D_HEADS), :] = dk[:, h * D_V:(h + 1) * D_V]
    dk16_ref[...] = dk.astype(BF16)
    dv = proj("dv")
    for h in range(D_HEADS):
        dv32_ref[pl.ds(h, tm, stride=D_HEADS), :] = dv[:, h * D_V:(h + 1) * D_V]
    dv16_ref[...] = dv.astype(BF16)
    fq_ref[...] = proj("fq").astype(BF16)
    fk = proj("fk")
    fk32_ref[...] = fk
    fk16_ref[...] = fk.astype(BF16)
    fv = proj("fv")
    fv32_ref[...] = fv
    fv16_ref[...] = fv.astype(BF16)
    z_ref[...] = proj("z")

    pre = _dot_nt(wg_ref[...], xb) + gb_ref[...]
    row = lax.broadcasted_iota(jnp.int32, pre.shape, 0)
    act = jnp.where(row < M_HEADS, pre, _log_sigmoid(pre))

    @pl.when(i % tiles_per_seq == 0)
    def _():
        carry_ref[...] = jnp.zeros_like(carry_ref)

    hi, mid, lo = _split3(act)
    tri = tri_ref[...]
    cs = _dot(hi, tri) + _dot(mid, tri) + _dot(lo, tri) + carry_ref[:, 0:1]
    carry_ref[...] = jnp.broadcast_to(cs[:, cs.shape[1] - 1:], carry_ref.shape)
    shifted = pltpu.roll(cs, 4, axis=0)
    gates_ref[...] = jnp.where((row >= 12), shifted, jnp.where(row < 12, act, 0.0))


def _in_proj(x2d, norm_g, w_main, w_gate_t, gate_bias, tri, tm, tiles_per_seq):
    n = x2d.shape[0]
    grid = (n // tm,)
    row_spec = lambda w: pl.BlockSpec((tm, w), lambda i: (i, 0))
    const = lambda shape: pl.BlockSpec(shape, lambda i: (0,) * len(shape))
    out_shape = [
        jax.ShapeDtypeStruct((n, M_WIDTH), F32),
        jax.ShapeDtypeStruct((n, M_WIDTH), F32),
        jax.ShapeDtypeStruct((n, D_WIDTH), BF16),
        jax.ShapeDtypeStruct((n * D_HEADS, D_V), F32),
        jax.ShapeDtypeStruct((n, D_WIDTH), BF16),
        jax.ShapeDtypeStruct((n * D_HEADS, D_V), F32),
        jax.ShapeDtypeStruct((n, D_WIDTH), BF16),
        jax.ShapeDtypeStruct((n, F_WIDTH), BF16),
        jax.ShapeDtypeStruct((n, F_WIDTH), F32),
        jax.ShapeDtypeStruct((n, F_WIDTH), BF16),
        jax.ShapeDtypeStruct((n, F_WIDTH), F32),
        jax.ShapeDtypeStruct((n, F_WIDTH), BF16),
        jax.ShapeDtypeStruct((n, MIX_WIDTH), F32),
        jax.ShapeDtypeStruct((GATE_ROWS, n), F32),
    ]
    out_specs = [pl.BlockSpec((tm * (s.shape[0] // n), s.shape[1]), lambda i: (i, 0)) for s in out_shape[:-1]]
    out_specs.append(pl.BlockSpec((GATE_ROWS, tm), lambda i: (0, i)))
    return pl.pallas_call(
        functools.partial(_in_proj_kernel, tiles_per_seq=tiles_per_seq),
        grid=grid,
        in_specs=[row_spec(D_MODEL), const((1, D_MODEL)), const((D_MODEL, MAIN_WIDTH)),
                  const((GATE_ROWS, D_MODEL)), const((GATE_ROWS, 1)), const((tm, tm))],
        out_specs=out_specs,
        out_shape=out_shape,
        scratch_shapes=[pltpu.VMEM((GATE_ROWS, LANES), F32)],
        compiler_params=_cparams(("arbitrary",)),
        name="in_proj",
    )(x2d, norm_g, w_main, w_gate_t, gate_bias, tri)


def _mlstm_kernel(um_ref, vm_ref, gates_ref, hist_ref, c0_ref, n0_ref, m0_ref,
                  cw_ref, cb_ref, wq_ref, wk_ref, mnorm_ref, mskip_ref,
                  hm_ref, c_out_ref, n_out_ref, m_out_ref,
                  ubuf, c_s, n_s, m_s):
    c = pl.program_id(1)

    @pl.when(c == 0)
    def _():
        ubuf[:, 0:8, :] = hist_ref[...]
        c_s[...] = c0_ref[...]
        n_s[...] = n0_ref[...]
        m_s[...] = m0_ref[...]

    for i in range(um_ref.shape[0]):
        _mlstm_chunk(um_ref.at[i], vm_ref.at[i], gates_ref.at[i], cw_ref, cb_ref, wq_ref, wk_ref,
                     mnorm_ref, mskip_ref, hm_ref.at[i], ubuf.at[i], c_s.at[i], n_s.at[i], m_s.at[i])

    @pl.when(c == pl.num_programs(1) - 1)
    def _():
        c_out_ref[...] = c_s[...]
        n_out_ref[...] = n_s[...]
        m_out_ref[...] = m_s[...]


def _mlstm_chunk(um_ref, vm_ref, gates_ref, cw_ref, cb_ref, wq_ref, wk_ref, mnorm_ref, mskip_ref,
                 hm_ref, ubuf, c_s, n_s, m_s):
    L = M_CHUNK
    ubuf[8:8 + L, :] = um_ref[...]
    conv = cb_ref[...]
    for j in range(CONV_W):
        conv = conv + ubuf[5 + j:5 + j + L, :] * cw_ref[j:j + 1, :]
    ubuf[0:8, :] = ubuf[L:L + 8, :]
    xc = _silu(conv)
    xb = xc.astype(BF16)
    q_all = _dot(xb, wq_ref[...])
    k_all = _dot(xb, wk_ref[...])
    v_all = vm_ref[...]

    gates = gates_ref[...]
    lane = lax.broadcasted_iota(jnp.int32, gates.shape, 1)
    bcum = gates
    shift = 1
    while shift < L:
        bcum = bcum + jnp.where(lane >= shift, pltpu.roll(bcum, shift, axis=1), 0.0)
        shift *= 2

    ri = lax.broadcasted_iota(jnp.int32, (L, L), 0)
    ci = lax.broadcasted_iota(jnp.int32, (L, L), 1)
    causal = ci <= ri
    eye = ci == ri

    outs = []
    m_prev_all = m_s[...]
    m_new_all = m_prev_all
    lane1 = lax.broadcasted_iota(jnp.int32, m_prev_all.shape, 1)
    for h in range(M_HEADS):
        sl = slice(h * M_DIM, (h + 1) * M_DIM)
        q_h, k_h, v_h = q_all[:, sl], k_all[:, sl], v_all[:, sl]
        li_row = gates[h:h + 1, :]
        lf_row = gates[M_HEADS + h:M_HEADS + h + 1, :]
        a_row = li_row - bcum[M_HEADS + h:M_HEADS + h + 1, :]
        m_prev = m_prev_all[:, h:h + 1]
        c_prev = c_s[h]
        n_prev = n_s[h:h + 1, :]

        a_mat = jnp.where(causal, a_row, NEG_INF)
        g = jnp.maximum(jnp.max(a_mat, axis=1, keepdims=True), m_prev)
        b_col = jnp.sum(jnp.where(causal, lf_row, 0.0), axis=1, keepdims=True)
        w = jnp.exp(a_mat - g)
        w_inter = jnp.exp(m_prev - g)
        qb = q_h.astype(BF16)
        s = _dot_nt(qb, k_h.astype(BF16)) * w
        num = w_inter * _dot(qb, c_prev.astype(BF16)) + _dot(s.astype(BF16), v_h.astype(BF16))
        den = (w_inter * jnp.sum(q_h * n_prev, axis=1, keepdims=True)
               + jnp.sum(s, axis=1, keepdims=True))
        m_t = b_col + g
        hh = num / jnp.maximum(jnp.abs(den), jnp.exp(-m_t))
        hn = hh * lax.rsqrt(jnp.mean(hh * hh, axis=1, keepdims=True) + EPS)
        outs.append(hn * mnorm_ref[:, sl] + mskip_ref[:, sl] * xc[:, sl])

        g_last = g[L - 1:L, :]
        decay = jnp.exp(m_prev - g_last)
        w_last_row = w[L - 1:L, :]
        w_last_col = jnp.sum(jnp.where(eye, w_last_row, 0.0), axis=1, keepdims=True)
        kw = k_h * w_last_col
        c_s[h] = decay * c_prev + _dot_tn(kw.astype(BF16), v_h.astype(BF16))
        n_s[h:h + 1, :] = decay * n_prev + jnp.sum(kw, axis=0, keepdims=True)
        m_new_all = jnp.where(lane1 == h, m_t[L - 1:L, :], m_new_all)
    m_s[...] = m_new_all
    hm_ref[...] = jnp.concatenate(outs, axis=1)


def _mlstm(um, vm, gates, hist8, c0, n0, m0, cw, cb, wq_bd, wk_bd, mnorm, mskip, seqs_per_step):
    n_seq, seq, _ = um.shape
    L = M_CHUNK
    S = seqs_per_step
    tok = pl.BlockSpec((S, L, M_WIDTH), lambda b, c: (b, c, 0))
    const = lambda shape: pl.BlockSpec(shape, lambda b, c: (0,) * len(shape))
    per_seq = lambda shape: pl.BlockSpec((S,) + shape, lambda b, c: (b,) + (0,) * len(shape))
    return pl.pallas_call(
        _mlstm_kernel,
        grid=(n_seq // S, seq // L),
        in_specs=[tok, tok,
                  pl.BlockSpec((S, GATE_ROWS, L), lambda b, c: (b, 0, c)),
                  per_seq((8, M_WIDTH)), per_seq((M_HEADS, M_DIM, M_DIM)),
                  per_seq((M_HEADS, M_DIM)), per_seq((1, LANES)),
                  const((CONV_W, M_WIDTH)), const((1, M_WIDTH)),
                  const((M_WIDTH, M_WIDTH)), const((M_WIDTH, M_WIDTH)),
                  const((1, M_WIDTH)), const((1, M_WIDTH))],
        out_specs=[tok, per_seq((M_HEADS, M_DIM, M_DIM)), per_seq((M_HEADS, M_DIM)),
                   per_seq((1, LANES))],
        out_shape=[jax.ShapeDtypeStruct((n_seq, seq, M_WIDTH), F32),
                   jax.ShapeDtypeStruct((n_seq, M_HEADS, M_DIM, M_DIM), F32),
                   jax.ShapeDtypeStruct((n_seq, M_HEADS, M_DIM), F32),
                   jax.ShapeDtypeStruct((n_seq, 1, LANES), F32)],
        scratch_shapes=[pltpu.VMEM((S, L + 8, M_WIDTH), F32),
                        pltpu.VMEM((S, M_HEADS, M_DIM, M_DIM), F32),
                        pltpu.VMEM((S, M_HEADS, M_DIM), F32),
                        pltpu.VMEM((S, 1, LANES), F32)],
        compiler_params=_cparams(("arbitrary", "arbitrary")),
        name="mlstm",
    )(um, vm, gates, hist8, c0, n0, m0, cw, cb, wq_bd, wk_bd, mnorm, mskip)


def _lambda(lq1_ref, lk1_ref, lq2_ref, lk2_ref, lam_init):
    e1 = jnp.exp(jnp.sum(lq1_ref[...] * lk1_ref[...], axis=1, keepdims=True))
    e2 = jnp.exp(jnp.sum(lq2_ref[...] * lk2_ref[...], axis=1, keepdims=True))
    return e1 - e2 + lam_init


def _attn_kernel(*refs, tq, tk, rc, fox, lam_init):
    if fox:
        q_ref, k_ref, v_ref, f_ref, o_ref, qs, s_a, s_b, p_a, p_b, al_a, al_b, m_s, l_s, acc = refs
    else:
        (q_ref, k_ref, v_ref, dn_ref, lq1_ref, lk1_ref, lq2_ref, lk2_ref,
         o_ref, qs, s_a, s_b, p_a, p_b, al_a, al_b, m_s, l_s, acc) = refs
    s_buf, p_buf, al_buf = (s_a, s_b), (p_a, p_b), (al_a, al_b)
    i = pl.program_id(2)
    j_last = ((i + 1) * tq - 1) // tk
    half = LANES // 2
    f_row = 12 + 2 * pl.program_id(1)

    q = q_ref[...].astype(F32)
    lane = lax.broadcasted_iota(jnp.int32, q.shape, 1)
    qs[0:tq, :] = jnp.where(lane < half, q, 0.0).astype(BF16)
    qs[tq:2 * tq, :] = jnp.where(lane >= half, q, 0.0).astype(BF16)
    m_s[...] = jnp.full_like(m_s, NEG_INF)
    l_s[...] = jnp.zeros_like(l_s)
    acc[...] = jnp.zeros_like(acc)
    p_b[...] = jnp.zeros_like(p_b)
    al_b[...] = jnp.ones_like(al_b)

    diag = (lax.broadcasted_iota(jnp.int32, (rc, tk), 1) - lax.broadcasted_iota(jnp.int32, (rc, tk), 0))

    def scores(t, slot):
        k0 = pl.multiple_of(t * tk, tk)
        s_buf[slot][...] = _dot_nt(qs[...], k_ref[pl.ds(k0, tk), :])

    def values(t, slot):
        k0 = pl.multiple_of(jnp.maximum(t, 0) * tk, tk)
        acc[...] = al_buf[slot][...] * acc[...] + _dot(p_buf[slot][...], v_ref[pl.ds(k0, tk), :])

    def softmax(t, slot, masked):
        k0 = pl.multiple_of(t * tk, tk)
        if fox:
            f_top = f_ref[pl.ds(f_row, 1), pl.ds(k0, tk)]
            f_bot = f_ref[pl.ds(f_row + 1, 1), pl.ds(k0, tk)]
        for c in range(2 * tq // rc):
            rows = slice(c * rc, (c + 1) * rc)
            s = s_buf[slot][rows, :]
            if fox:
                s = s - (f_top if c * rc < tq else f_bot)
            if masked:
                s = jnp.where(diag <= i * tq + (c * rc) % tq - k0, s, NEG_INF)
            m_prev = m_s[rows, :]
            m_new = jnp.maximum(m_prev, jnp.max(s, axis=1, keepdims=True))
            alpha = jnp.exp(m_prev - m_new)
            p = jnp.exp(s - jnp.concatenate([m_new] * (tk // LANES), axis=1))
            l_s[rows, :] = alpha * l_s[rows, :] + jnp.sum(p, axis=1, keepdims=True)
            m_s[rows, :] = m_new
            al_buf[slot][rows, :] = alpha
            p_buf[slot][rows, :] = p.astype(BF16)

    def step(t, slot):
        scores(t + 1, 1 - slot)
        values(t - 1, 1 - slot)
        softmax(t, slot, False)

    def last(slot):
        values(j_last - 1, 1 - slot)
        softmax(j_last, slot, True)
        values(j_last, slot)

    scores(0, 0)

    def pair(u, carry):
        step(2 * u, 0)
        step(2 * u + 1, 1)
        return carry

    lax.fori_loop(0, j_last // 2, pair, 0)

    @pl.when(j_last % 2 == 1)
    def _():
        step(j_last - 1, 0)
        last(1)

    @pl.when(j_last % 2 == 0)
    def _():
        last(0)

    o = acc[...] / l_s[...]
    top, bot = o[0:tq, :], o[tq:2 * tq, :]
    if fox:
        o_ref[...] = jnp.where(lane < half, top, bot)
    else:
        lam = _lambda(lq1_ref, lk1_ref, lq2_ref, lk2_ref, lam_init)
        od = top - lam * bot
        od = od * lax.rsqrt(jnp.mean(od * od, axis=1, keepdims=True) + EPS)
        o_ref[...] = od * dn_ref[...] * (1.0 - lam_init)


def _prompt_attn(q, k, v, extra, n_seq, seq, tq, tk, rc, fox, lam_init):
    n, width = q.shape
    groups = width // LANES
    nq = seq // tq
    q_spec = pl.BlockSpec((tq, LANES), lambda b, h, i: (b * nq + i, h))
    kv_spec = pl.BlockSpec((seq, LANES), lambda b, h, i: (b, h))
    if fox:
        extra_specs = [pl.BlockSpec((GATE_ROWS, seq), lambda b, h, i: (0, b))]
    else:
        lam_spec = pl.BlockSpec((1, D_QK), lambda b, h, i: (0, 0))
        extra_specs = [pl.BlockSpec((1, LANES), lambda b, h, i: (0, h))] + [lam_spec] * 4
    return pl.pallas_call(
        functools.partial(_attn_kernel, tq=tq, tk=tk, rc=rc, fox=fox, lam_init=lam_init),
        grid=(n_seq, groups, nq),
        in_specs=[q_spec, kv_spec, kv_spec] + extra_specs,
        out_specs=q_spec,
        out_shape=jax.ShapeDtypeStruct((n, width), F32),
        scratch_shapes=[pltpu.VMEM((2 * tq, LANES), BF16)]
                       + [pltpu.VMEM((2 * tq, tk), F32)] * 2
                       + [pltpu.VMEM((2 * tq, tk), BF16)] * 2
                       + [pltpu.VMEM((2 * tq, LANES), F32)] * 2
                       + [pltpu.VMEM((2 * tq, LANES), F32)] * 3,
        compiler_params=_cparams(("arbitrary",) * 3),
        name="fox_attn" if fox else "diff_attn",
    )(q, k, v, *extra)


D_ROWS = PAGE_SIZE * D_HEADS


def _rows_from_segments(row_vec, n_seg, r):
    out = jnp.zeros((8, LANES), F32)
    for s in range(n_seg):
        seg = jnp.broadcast_to(row_vec[:, s * LANES:(s + 1) * LANES], (8, LANES))
        out = jnp.where(lax.shift_right_logical(r, 1) == s, seg, out)
    return out


def _decode_kernel(pt_ref, *refs, pages_per_step, lam_init):
    G = pages_per_step
    (dq_ref, fq_ref, dkn_ref, dvn_ref, fkn_ref, fvn_ref, gn_ref,
     dn_ref, lq1_ref, lk1_ref, lq2_ref, lk2_ref, tri_ref) = refs[:13]
    page_refs = refs[13:13 + 5 * G]
    od_ref, of_ref = refs[13 + 5 * G:15 + 5 * G]
    qd_s, qf_s, md_s, ld_s, accd, mf_s, lf_s, accf, fcar = refs[15 + 5 * G:]
    j = pl.program_id(1)
    r8 = lax.broadcasted_iota(jnp.int32, (8, LANES), 0)
    l8 = lax.broadcasted_iota(jnp.int32, (8, LANES), 1)
    half_sel = lax.shift_right_logical(l8, 6) == (r8 & 1)

    def head_cols(shape):
        r = lax.broadcasted_iota(jnp.int32, shape, 0)
        c = lax.broadcasted_iota(jnp.int32, shape, 1)
        return lax.shift_right_logical(c, 6) == r

    @pl.when(j == 0)
    def _():
        qd = _rows_from_segments(dq_ref[0].astype(F32), D_HEADS, r8)
        qd_s[...] = jnp.where(half_sel, qd, 0.0).astype(BF16)
        qf = jnp.broadcast_to(fq_ref[0].astype(F32), (8, F_WIDTH))
        qf_s[...] = jnp.where(head_cols(qf.shape), qf, 0.0).astype(BF16)
        md_s[...] = jnp.full_like(md_s, NEG_INF)
        mf_s[...] = jnp.full_like(mf_s, NEG_INF)
        ld_s[...] = jnp.zeros_like(ld_s)
        lf_s[...] = jnp.zeros_like(lf_s)
        accd[...] = jnp.zeros_like(accd)
        accf[...] = jnp.zeros_like(accf)
        fcar[...] = jnp.zeros_like(fcar)

    def online(s, v_refs, width, pv_dot, m_ref, l_ref, acc_ref):
        m_prev = m_ref[...]
        m_new = jnp.maximum(m_prev, jnp.max(s, axis=1, keepdims=True))
        alpha = jnp.exp(m_prev - m_new)
        p = jnp.exp(s - m_new)
        l_ref[...] = alpha * l_ref[...] + jnp.sum(p, axis=1, keepdims=True)
        pb = p.astype(BF16)
        pv = pv_dot(pb[:, 0:width], v_refs[0][0, 0].astype(BF16))
        for g in range(1, G):
            pv = pv + pv_dot(pb[:, g * width:(g + 1) * width], v_refs[g][0, 0].astype(BF16))
        acc_ref[...] = alpha * acc_ref[...] + pv
        m_ref[...] = m_new

    kd_refs = page_refs[0::5]
    vd_refs = page_refs[1::5]
    kf_refs = page_refs[2::5]
    vf_refs = page_refs[3::5]
    fl_refs = page_refs[4::5]

    qd = qd_s[...]
    s_d = jnp.concatenate([_dot_nt(qd, kd_refs[g][0, 0].astype(BF16)) for g in range(G)], axis=1)
    rd = lax.broadcasted_iota(jnp.int32, s_d.shape, 0)
    cd = lax.broadcasted_iota(jnp.int32, s_d.shape, 1)
    s_d = jnp.where((cd & (D_HEADS - 1)) == lax.shift_right_logical(rd, 1), s_d, NEG_INF)
    online(s_d, vd_refs, D_ROWS, _dot, md_s, ld_s, accd)

    qf = qf_s[...]
    tri = tri_ref[...]
    carry = fcar[...]
    parts = []
    for g in range(G):
        fl = fl_refs[g][0, 0]
        hi, mid, lo = _split3(fl)
        cum = _dot(hi, tri) + _dot(mid, tri) + _dot(lo, tri) + carry
        carry = carry + jnp.sum(fl, axis=1, keepdims=True)
        parts.append(_dot(qf, kf_refs[g][0, 0].astype(BF16)) - cum)
    fcar[...] = carry
    online(jnp.concatenate(parts, axis=1), vf_refs, PAGE_SIZE, _dot_nt, mf_s, lf_s, accf)

    @pl.when(j == pl.num_programs(1) - 1)
    def _():
        def add_new(q_s, k_new, v_new, bias, m_ref, l_ref, acc_ref):
            s = jnp.sum(q_s[...].astype(F32) * k_new, axis=1, keepdims=True) - bias
            m_prev = m_ref[...]
            m_new = jnp.maximum(m_prev, s)
            alpha = jnp.exp(m_prev - m_new)
            p = jnp.exp(s - m_new)
            l_new = alpha * l_ref[...] + p
            return (alpha * acc_ref[...] + p * v_new) / l_new

        od_all = add_new(qd_s, _rows_from_segments(dkn_ref[0], D_HEADS, r8),
                         _rows_from_segments(dvn_ref[0], D_HEADS, r8), 0.0, md_s, ld_s, accd)
        f_new = fcar[...] + gn_ref[0]
        of_all = add_new(qf_s, fkn_ref[0], fvn_ref[0], f_new, mf_s, lf_s, accf)

        lam = _lambda(lq1_ref, lk1_ref, lq2_ref, lk2_ref, lam_init)
        parts_d = []
        for h in range(D_HEADS):
            seg = od_all[2 * h:2 * h + 1, :] - lam * od_all[2 * h + 1:2 * h + 2, :]
            parts_d.append(seg * lax.rsqrt(jnp.mean(seg * seg, axis=1, keepdims=True) + EPS))
        od_ref[0] = jnp.concatenate(parts_d, axis=1) * dn_ref[...] * (1.0 - lam_init)
        of_ref[0] = jnp.sum(jnp.where(head_cols(of_all.shape), of_all, 0.0), axis=0, keepdims=True)


def _decode_attn(page_table, layer, dq, fq, dk_new, dv_new, fk_new, fv_new, g_new,
                 d_norm, lq1, lk1, lq2, lk2, tri, c_dk, c_dv, c_fk, c_fv, c_flt,
                 pages_per_step, lam_init):
    nb, n_pages = page_table.shape
    G = pages_per_step
    steps = n_pages // G
    per_b = lambda w: pl.BlockSpec((1, 1, w), lambda b, j, pt: (b, 0, 0))
    const = lambda shape: pl.BlockSpec(shape, lambda b, j, pt: (0,) * len(shape))
    in_specs = [per_b(D_WIDTH), per_b(F_WIDTH), per_b(D_WIDTH), per_b(D_WIDTH),
                per_b(F_WIDTH), per_b(F_WIDTH),
                pl.BlockSpec((1, 8, 1), lambda b, j, pt: (b, 0, 0)),
                const((1, D_WIDTH))] + [const((1, D_QK))] * 4 + [const((PAGE_SIZE, PAGE_SIZE))]
    args = [dq, fq, dk_new, dv_new, fk_new, fv_new, g_new, d_norm, lq1, lk1, lq2, lk2, tri]
    for g in range(G):
        def page(rows, w, g=g):
            return pl.BlockSpec((1, 1, rows, w),
                                lambda b, j, pt: (layer, pt[b, j * G + g], 0, 0))
        in_specs += [page(D_ROWS, LANES), page(D_ROWS, LANES),
                     page(F_WIDTH, PAGE_SIZE), page(F_WIDTH, PAGE_SIZE), page(8, PAGE_SIZE)]
        args += [c_dk, c_dv, c_fk, c_fv, c_flt]
    grid_spec = pltpu.PrefetchScalarGridSpec(
        num_scalar_prefetch=1,
        grid=(nb, steps),
        in_specs=in_specs,
        out_specs=[per_b(D_WIDTH), per_b(F_WIDTH)],
        scratch_shapes=[pltpu.VMEM((8, LANES), BF16), pltpu.VMEM((8, F_WIDTH), BF16),
                        pltpu.VMEM((8, 1), F32), pltpu.VMEM((8, 1), F32),
                        pltpu.VMEM((8, LANES), F32),
                        pltpu.VMEM((8, 1), F32), pltpu.VMEM((8, 1), F32),
                        pltpu.VMEM((8, F_WIDTH), F32),
                        pltpu.VMEM((8, 1), F32)])
    return pl.pallas_call(
        functools.partial(_decode_kernel, pages_per_step=G, lam_init=lam_init),
        grid_spec=grid_spec,
        out_shape=[jax.ShapeDtypeStruct((nb, 1, D_WIDTH), F32),
                   jax.ShapeDtypeStruct((nb, 1, F_WIDTH), F32)],
        compiler_params=_cparams(("arbitrary", "arbitrary")),
        name="decode_attn",
    )(page_table, *args)


def _out_proj_kernel(x_ref, hm_ref, od_ref, of_ref, z_ref, w_ref, g_ref, o_ref):
    h = jnp.concatenate([hm_ref[...], od_ref[...], of_ref[...]], axis=1) * _silu(z_ref[...])
    y = _dot(h.astype(BF16), w_ref[...])
    yn = y * lax.rsqrt(jnp.mean(y * y, axis=-1, keepdims=True) + EPS) * g_ref[...]
    o_ref[...] = x_ref[...] + yn


def _out_proj(x2d, hm, od, of, z, w_out, norm_g, tm):
    n = x2d.shape[0]
    row_spec = lambda w: pl.BlockSpec((tm, w), lambda i: (i, 0))
    const = lambda shape: pl.BlockSpec(shape, lambda i: (0,) * len(shape))
    return pl.pallas_call(
        _out_proj_kernel,
        grid=(n // tm,),
        in_specs=[row_spec(D_MODEL), row_spec(M_WIDTH), row_spec(D_WIDTH), row_spec(F_WIDTH),
                  row_spec(MIX_WIDTH), const((MIX_WIDTH, D_MODEL)), const((1, D_MODEL))],
        out_specs=row_spec(D_MODEL),
        out_shape=jax.ShapeDtypeStruct((n, D_MODEL), F32),
        compiler_params=_cparams(("arbitrary",)),
        name="out_proj",
    )(x2d, hm, od, of, z, w_out, norm_g)


def _block_diag(w):
    hh, d, _ = w.shape
    out = jnp.zeros((hh * d, hh * d), w.dtype)
    for h in range(hh):
        out = out.at[h * d:(h + 1) * d, h * d:(h + 1) * d].set(w[h])
    return out


def _layer_weights(l, w_in, b_i, b_f, b_fox, wq_m, wk_m):
    idx = [0]
    for s in SPLITS:
        idx.append(idx[-1] + s)
    cols = lambda k: w_in[l][:, idx[k]:idx[k + 1]]
    u_m, v_m, ig, fg, dq, dk, dv, fq, fk, fv, ffg, z = [cols(k) for k in range(len(SPLITS))]
    w_main = jnp.concatenate([u_m, v_m, dq * (D_QK ** -0.5), dk, dv, fq * (F_DIM ** -0.5), fk, fv, z],
                             axis=1).astype(BF16)
    w_gate_t = jnp.concatenate([ig, fg, ffg, jnp.zeros((D_MODEL, GATE_ROWS - 12), F32)],
                               axis=1).T.astype(BF16)
    gate_bias = jnp.concatenate([b_i[l], b_f[l], b_fox[l], jnp.zeros((GATE_ROWS - 12,), F32)]
                                ).astype(F32).reshape(GATE_ROWS, 1)
    wq_bd = _block_diag(wq_m[l]).astype(BF16)
    wk_bd = (_block_diag(wk_m[l]) * (M_DIM ** -0.5)).astype(BF16)
    return w_main, w_gate_t, gate_bias, wq_bd, wk_bd


def kernel(x_prompt, x_sample, cache_dk, cache_dv, cache_fk, cache_fv, cache_flogf, state_mconv, state_mC, state_mn, state_mm, page_table, norm_pre, norm_post, w_in, conv_w, conv_b, wq_m, wk_m, b_i, b_f, m_norm, m_skip, lam_q1, lam_k1, lam_q2, lam_k2, d_norm, b_fox, w_out):
    bp, seq, _ = x_prompt.shape
    nb = x_sample.shape[0]
    n_pool = cache_dk.shape[1]
    n_p = bp * seq
    tm_p = 512
    tq, tk = 256, 512
    rc = 64
    pages_per_step = 16
    L = M_CHUNK

    xp = x_prompt.reshape(n_p, D_MODEL)
    ns = LANES
    xs = jnp.pad(x_sample.reshape(nb, D_MODEL), ((0, ns - nb), (0, 0)))
    c_dk = cache_dk.reshape(DEPTH, n_pool, D_ROWS, LANES)
    c_dv = cache_dv.reshape(DEPTH, n_pool, D_ROWS, LANES)
    c_fk = jnp.transpose(cache_fk, (0, 1, 3, 4, 2)).reshape(DEPTH, n_pool, F_WIDTH, PAGE_SIZE)
    c_fv = jnp.transpose(cache_fv, (0, 1, 3, 4, 2)).reshape(DEPTH, n_pool, F_WIDTH, PAGE_SIZE)
    c_flt = jnp.pad(jnp.swapaxes(cache_flogf.astype(F32), 2, 3), ((0, 0), (0, 0), (0, 8 - F_HEADS), (0, 0)))

    tri_p = (lax.broadcasted_iota(jnp.int32, (tm_p, tm_p), 0)
             <= lax.broadcasted_iota(jnp.int32, (tm_p, tm_p), 1)).astype(BF16)
    tri_s = jnp.eye(ns, dtype=BF16)
    tri_page = tri_p[:PAGE_SIZE, :PAGE_SIZE]

    zero_hist = jnp.zeros((bp, 8, M_WIDTH), F32)
    zero_c = jnp.zeros((bp, M_HEADS, M_DIM, M_DIM), F32)
    zero_n = jnp.zeros((bp, M_HEADS, M_DIM), F32)
    zero_m = jnp.zeros((bp, 1, LANES), F32)

    new_p, new_s = [], []
    for l in range(DEPTH):
        lam_init = 0.8 - 0.6 * math.exp(-0.3 * l)
        w_main, w_gate_t, gate_bias, wq_bd, wk_bd = _layer_weights(l, w_in, b_i, b_f, b_fox, wq_m, wk_m)
        w_out_l = w_out[l].astype(BF16)
        g_pre = norm_pre[l].reshape(1, D_MODEL)
        g_post = norm_post[l].reshape(1, D_MODEL)
        cw, cb = conv_w[l], conv_b[l].reshape(1, M_WIDTH)
        mnorm, mskip = m_norm[l].reshape(1, M_WIDTH), m_skip[l].reshape(1, M_WIDTH)
        dn = d_norm[l].reshape(1, D_WIDTH)
        lams = [a[l].reshape(1, D_QK).astype(F32) for a in (lam_q1, lam_k1, lam_q2, lam_k2)]

        (um, vm, dq, dk32, dk16, dv32, dv16, fq, fk32, fk16, fv32, fv16, z, gates) = _in_proj(
            xp, g_pre, w_main, w_gate_t, gate_bias, tri_p, tm_p, seq // tm_p)
        gates_seq = gates.reshape(GATE_ROWS, bp, seq).transpose(1, 0, 2)
        hm, c_fin, n_fin, m_fin = _mlstm(um.reshape(bp, seq, M_WIDTH), vm.reshape(bp, seq, M_WIDTH),
                                         gates_seq, zero_hist, zero_c, zero_n, zero_m,
                                         cw, cb, wq_bd, wk_bd, mnorm, mskip, 1)
        hm = hm.reshape(n_p, M_WIDTH)
        od = _prompt_attn(dq, dk16, dv16, [dn] + lams, bp, seq, tq, tk, rc, False, lam_init)
        of = _prompt_attn(fq, fk16, fv16, [gates], bp, seq, tq, tk, rc, True, lam_init)
        xp = _out_proj(xp, hm, od, of, z, w_out_l, g_post, tm_p)
        new_p.append((dk32.reshape(bp, seq, D_HEADS, 2 * D_QK), dv32.reshape(bp, seq, D_HEADS, D_V),
                      fk32.reshape(bp, seq, F_HEADS, F_DIM), fv32.reshape(bp, seq, F_HEADS, F_DIM),
                      gates[8:12].T.reshape(bp, seq, F_HEADS),
                      um.reshape(bp, seq, M_WIDTH)[:, seq - (CONV_W - 1):],
                      c_fin, n_fin, m_fin[:, 0, :M_HEADS]))

        outs_s = _in_proj(xs, g_pre, w_main, w_gate_t, gate_bias, tri_s, ns, 1)
        (um, vm, dq, dk32, _, dv32, _, fq, fk32, _, fv32, _, z, gates) = [
            a[:, :nb] if k == 13 else a[:nb * (a.shape[0] // ns)] for k, a in enumerate(outs_s)]
        z = outs_s[12]
        dk32, dv32 = dk32.reshape(nb, D_WIDTH), dv32.reshape(nb, D_WIDTH)
        pad_tok = lambda a: jnp.pad(a[:, None, :], ((0, 0), (0, L - 1), (0, 0)))
        noop = jnp.where(jnp.arange(GATE_ROWS) < M_HEADS, NEG_INF, 0.0).astype(F32)
        gates_pad = jnp.concatenate(
            [gates.T[:, :, None], jnp.broadcast_to(noop[None, :, None], (nb, GATE_ROWS, L - 1))], axis=2)
        hist = state_mconv[l].astype(F32)
        hist8 = jnp.pad(hist, ((0, 0), (8 - (CONV_W - 1), 0), (0, 0)))
        m0 = jnp.pad(state_mm[l].astype(F32), ((0, 0), (0, LANES - M_HEADS)))[:, None, :]
        hm, c_fin, n_fin, m_fin = _mlstm(pad_tok(um), pad_tok(vm), gates_pad, hist8,
                                         state_mC[l].astype(F32), state_mn[l].astype(F32), m0,
                                         cw, cb, wq_bd, wk_bd, mnorm, mskip, 1)
        hm = hm[:, 0]
        flog_new = gates[8:12].T
        g_new = jnp.pad(flog_new, ((0, 0), (0, 8 - F_HEADS)))[:, :, None]
        od, of = _decode_attn(page_table, l, dq[:, None, :], fq[:, None, :], dk32[:, None, :],
                              dv32[:, None, :], fk32[:, None, :], fv32[:, None, :], g_new,
                              dn, *lams, tri_page, c_dk, c_dv, c_fk, c_fv, c_flt, pages_per_step, lam_init)
        pad_rows = lambda a: jnp.pad(a, ((0, ns - nb), (0, 0)))
        xs = _out_proj(xs, pad_rows(hm), pad_rows(od[:, 0]), pad_rows(of[:, 0]), z, w_out_l, g_post, ns)
        new_s.append((dk32.reshape(nb, 1, D_HEADS, 2 * D_QK), dv32.reshape(nb, 1, D_HEADS, D_V),
                      fk32.reshape(nb, 1, F_HEADS, F_DIM), fv32.reshape(nb, 1, F_HEADS, F_DIM),
                      flog_new.reshape(nb, 1, F_HEADS),
                      jnp.concatenate([hist[:, 1:], um[:, None, :]], axis=1),
                      c_fin, n_fin, m_fin[:, 0, :M_HEADS]))

    stack = lambda states, k: jnp.stack([st[k] for st in states])
    return ((xp.reshape(bp, seq, D_MODEL), xs[:nb].reshape(nb, 1, D_MODEL))
            + tuple(stack(new_p, k) for k in range(9))
            + tuple(stack(new_s, k) for k in range(9)))
```

```python
import functools
import math

import jax
import jax.numpy as jnp
from jax import lax
from jax.experimental import pallas as pl
from jax.experimental.pallas import tpu as pltpu

D_MODEL = 1024
DEPTH = 4
PAGE_SIZE = 128
M_HEADS = 4
M_DIM = 64
M_WIDTH = M_HEADS * M_DIM
CONV_W = 4
M_CHUNK = 128
D_HEADS = 4
D_QK = 64
D_V = 2 * D_QK
D_WIDTH = D_HEADS * D_V
F_HEADS = 4
F_DIM = 64
F_WIDTH = F_HEADS * F_DIM
MIX_WIDTH = M_WIDTH + D_WIDTH + F_WIDTH
EPS = 1e-6
SPLITS = (M_WIDTH, M_WIDTH, M_HEADS, M_HEADS, D_WIDTH, D_WIDTH, D_WIDTH,
          F_WIDTH, F_WIDTH, F_WIDTH, F_HEADS, MIX_WIDTH)

LANES = 128
GATE_ROWS = 16
VMEM_LIMIT = 56 * 1024 * 1024

F32 = jnp.float32
BF16 = jnp.bfloat16
NEG_INF = float("-inf")

_MAIN = (("um", M_WIDTH), ("vm", M_WIDTH), ("dq", D_WIDTH), ("dk", D_WIDTH), ("dv", D_WIDTH),
         ("fq", F_WIDTH), ("fk", F_WIDTH), ("fv", F_WIDTH), ("z", MIX_WIDTH))
_MAIN_OFF = {}
_o = 0
for _n, _w in _MAIN:
    _MAIN_OFF[_n] = (_o, _w)
    _o += _w
MAIN_WIDTH = _o


def _cparams(sem):
    return pltpu.CompilerParams(dimension_semantics=sem, vmem_limit_bytes=VMEM_LIMIT)


def _log_sigmoid(x):
    return jnp.minimum(x, 0.0) - jnp.log1p(jnp.exp(-jnp.abs(x)))


def _silu(x):
    return x * (1.0 / (1.0 + jnp.exp(-x)))


def _dot(a, b):
    return jnp.dot(a, b, preferred_element_type=F32)


def _dot_nt(a, b):
    return lax.dot_general(a, b, (((1,), (1,)), ((), ())), preferred_element_type=F32)


def _split3(x):
    hi = x.astype(BF16)
    r1 = x - hi.astype(F32)
    mid = r1.astype(BF16)
    lo = (r1 - mid.astype(F32)).astype(BF16)
    return hi, mid, lo


def _in_proj_kernel(x_ref, g_ref, w_ref, wvt_ref, wg_ref, gb_ref, tri_ref,
                    um_ref, vmt_ref, dq_ref, dk32_ref, dk16_ref, dv32_ref, dv16_ref,
                    fq_ref, fk32_ref, fk16_ref, fv32_ref, fv16_ref, z_ref, gates_ref,
                    carry_ref, *, tiles_per_seq):
    i = pl.program_id(0)
    x = x_ref[...]
    xn = x * lax.rsqrt(jnp.mean(x * x, axis=-1, keepdims=True) + EPS) * g_ref[...]
    xb = xn.astype(BF16)

    def proj(name):
        off, width = _MAIN_OFF[name]
        return _dot(xb, w_ref[:, off:off + width])

    um_ref[...] = proj("um")
    vmt_ref[...] = _dot_nt(wvt_ref[...], xb)
    dq_ref[...] = proj("dq").astype(BF16)
    dk = proj("dk")
    tm = x.shape[0]
    for h in range(D_HEADS):
        dk32_ref[pl.ds(h, tm, stride=D_HEADS), :] = dk[:, h * D_V:(h + 1) * D_V]
    dk16_ref[...] = dk.astype(BF16)
    dv = proj("dv")
    for h in range(D_HEADS):
        dv32_ref[pl.ds(h, tm, stride=D_HEADS), :] = dv[:, h * D_V:(h + 1) * D_V]
    dv16_ref[...] = dv.astype(BF16)
    fq_ref[...] = proj("fq").astype(BF16)
    fk = proj("fk")
    fk32_ref[...] = fk
    fk16_ref[...] = fk.astype(BF16)
    fv = proj("fv")
    fv32_ref[...] = fv
    fv16_ref[...] = fv.astype(BF16)
    z_ref[...] = proj("z")

    pre = _dot_nt(wg_ref[...], xb) + gb_ref[...]
    row = lax.broadcasted_iota(jnp.int32, pre.shape, 0)
    act = jnp.where(row < M_HEADS, pre, _log_sigmoid(pre))

    @pl.when(i % tiles_per_seq == 0)
    def _():
        carry_ref[...] = jnp.zeros_like(carry_ref)

    hi, mid, lo = _split3(act)
    tri = tri_ref[...]
    cs = _dot(hi, tri) + _dot(mid, tri) + _dot(lo, tri) + carry_ref[:, 0:1]
    carry_ref[...] = jnp.broadcast_to(cs[:, cs.shape[1] - 1:], carry_ref.shape)
    shifted = pltpu.roll(cs, 4, axis=0)
    gates_ref[...] = jnp.where((row >= 12), shifted, jnp.where(row < 12, act, 0.0))


def _in_proj(x2d, norm_g, w_main, w_vm_t, w_gate_t, gate_bias, tri, tm, tiles_per_seq):
    n = x2d.shape[0]
    grid = (n // tm,)
    row_spec = lambda w: pl.BlockSpec((tm, w), lambda i: (i, 0))
    const = lambda shape: pl.BlockSpec(shape, lambda i: (0,) * len(shape))
    out_shape = [
        jax.ShapeDtypeStruct((n, M_WIDTH), F32),
        jax.ShapeDtypeStruct((M_WIDTH, n), F32),
        jax.ShapeDtypeStruct((n, D_WIDTH), BF16),
        jax.ShapeDtypeStruct((n * D_HEADS, D_V), F32),
        jax.ShapeDtypeStruct((n, D_WIDTH), BF16),
        jax.ShapeDtypeStruct((n * D_HEADS, D_V), F32),
        jax.ShapeDtypeStruct((n, D_WIDTH), BF16),
        jax.ShapeDtypeStruct((n, F_WIDTH), BF16),
        jax.ShapeDtypeStruct((n, F_WIDTH), F32),
        jax.ShapeDtypeStruct((n, F_WIDTH), BF16),
        jax.ShapeDtypeStruct((n, F_WIDTH), F32),
        jax.ShapeDtypeStruct((n, F_WIDTH), BF16),
        jax.ShapeDtypeStruct((n, MIX_WIDTH), F32),
        jax.ShapeDtypeStruct((GATE_ROWS, n), F32),
    ]
    out_specs = [pl.BlockSpec((tm * (s.shape[0] // n), s.shape[1]), lambda i: (i, 0)) for s in out_shape[:-1]]
    out_specs[1] = pl.BlockSpec((M_WIDTH, tm), lambda i: (0, i))
    out_specs.append(pl.BlockSpec((GATE_ROWS, tm), lambda i: (0, i)))
    return pl.pallas_call(
        functools.partial(_in_proj_kernel, tiles_per_seq=tiles_per_seq),
        grid=grid,
        in_specs=[row_spec(D_MODEL), const((1, D_MODEL)), const((D_MODEL, MAIN_WIDTH)),
                  const((M_WIDTH, D_MODEL)), const((GATE_ROWS, D_MODEL)), const((GATE_ROWS, 1)),
                  const((tm, tm))],
        out_specs=out_specs,
        out_shape=out_shape,
        scratch_shapes=[pltpu.VMEM((GATE_ROWS, LANES), F32)],
        compiler_params=_cparams(("arbitrary",)),
        name="in_proj",
    )(x2d, norm_g, w_main, w_vm_t, w_gate_t, gate_bias, tri)


def _mlstm_kernel(um_ref, vt_ref, gates_ref, hist_ref, ct0_ref, n0_ref, m0_ref,
                  tri_ref, cw_ref, cb_ref, wk_ref, wqt_ref, wkt_ref, mnorm_ref, mskip_ref,
                  hm_ref, ct_out_ref, n_out_ref, m_out_ref,
                  ubuf, ct_s, n_s, m_s):
    c = pl.program_id(1)
    L = M_CHUNK

    @pl.when(c == 0)
    def _():
        ubuf[0:8, :] = hist_ref[0]
        ct_s[...] = ct0_ref[0]
        n_s[...] = n0_ref[0]
        m_s[...] = m0_ref[0]

    ubuf[8:8 + L, :] = um_ref[0]
    conv = cb_ref[...]
    for j in range(CONV_W):
        conv = conv + ubuf[5 + j:5 + j + L, :] * cw_ref[j:j + 1, :]
    ubuf[0:8, :] = ubuf[L:L + 8, :]
    xc = _silu(conv)
    xb = xc.astype(BF16)
    k_all = _dot(xb, wk_ref[...]).astype(BF16)
    qt_all = _dot_nt(wqt_ref[...], xb)
    kt_all = _dot_nt(wkt_ref[...], xb)
    vt_all = vt_ref[...]

    gates = gates_ref[0][0:8, :]
    row8 = lax.broadcasted_iota(jnp.int32, gates.shape, 0)
    hi, mid, lo = _split3(jnp.where(row8 >= M_HEADS, gates, 0.0))
    tri = tri_ref[...]
    bcum = _dot(hi, tri) + _dot(mid, tri) + _dot(lo, tri)
    a_rows = gates - pltpu.roll(bcum, M_HEADS, axis=0)

    si = lax.broadcasted_iota(jnp.int32, (L, L), 0)
    ti = lax.broadcasted_iota(jnp.int32, (L, L), 1)
    causal = si <= ti
    eye = si == ti
    zeros_half = jnp.zeros((M_DIM, L), BF16)

    m_prev_all = m_s[...]
    m_new_all = m_prev_all
    lane1 = lax.broadcasted_iota(jnp.int32, m_prev_all.shape, 1)
    ct_prev_all = [ct_s[h] for h in range(M_HEADS)]
    n_prev_all = [n_s[h] for h in range(M_HEADS)]
    ct_new, n_new, hts = [], [], []
    for h in range(M_HEADS):
        rows = slice(h * M_DIM, (h + 1) * M_DIM)
        pair = slice((h // 2) * LANES, (h // 2 + 1) * LANES)
        qt_h = qt_all[rows, :]
        qt_b = qt_h.astype(BF16)
        kt_b = kt_all[rows, :].astype(BF16)
        vt_b = vt_all[rows, :].astype(BF16)
        m_prev = m_prev_all[:, h:h + 1]
        ct_prev = ct_prev_all[h]
        n_prev = n_prev_all[h]

        a_row = a_rows[h:h + 1, :]
        b_row = bcum[M_HEADS + h:M_HEADS + h + 1, :]
        a_col = jnp.sum(jnp.where(eye, a_row, 0.0), axis=1, keepdims=True)
        g_row = jnp.maximum(jnp.max(jnp.where(causal, a_col, NEG_INF), axis=0, keepdims=True), m_prev)
        m_row = b_row + g_row
        w_inter = jnp.exp(m_prev - g_row)
        wt = jnp.exp(jnp.where(causal, a_col - g_row, NEG_INF))
        qt_pad = jnp.concatenate([qt_b, zeros_half] if h % 2 == 0 else [zeros_half, qt_b], axis=0)
        st = _dot(k_all[:, pair], qt_pad) * wt
        num_t = _dot(vt_b, st.astype(BF16)) + _dot(ct_prev.astype(BF16), qt_b) * w_inter
        qn = jnp.sum(qt_h * n_prev, axis=0, keepdims=True)
        den = w_inter * qn + jnp.sum(st, axis=0, keepdims=True)
        ht = num_t / jnp.maximum(jnp.abs(den), jnp.exp(-m_row))
        hts.append(ht * lax.rsqrt(jnp.mean(ht * ht, axis=0, keepdims=True) + EPS))

        g_last = g_row[:, L - 1:L]
        decay = jnp.exp(m_prev - g_last)
        w_last = jnp.exp(a_row - g_last)
        vtw = (vt_all[rows, :] * w_last).astype(BF16)
        ct_new.append(decay * ct_prev + _dot_nt(vtw, kt_b))
        n_new.append(decay * n_prev + jnp.sum(kt_all[rows, :] * w_last, axis=1, keepdims=True))
        m_new_all = jnp.where(lane1 == h, m_row[:, L - 1:L], m_new_all)
    for h in range(M_HEADS):
        ct_s[h] = ct_new[h]
        n_s[h] = n_new[h]
    m_s[...] = m_new_all
    hn = jnp.concatenate(hts, axis=0).T
    hm_ref[0] = hn * mnorm_ref[...] + mskip_ref[...] * xc

    @pl.when(c == pl.num_programs(1) - 1)
    def _():
        ct_out_ref[0] = ct_s[...]
        n_out_ref[0] = n_s[...]
        m_out_ref[0] = m_s[...]


def _mlstm(um, vm_t, gates, hist8, ct0, n0, m0, tri, cw, cb, wq_bd, wk_bd, mnorm, mskip):
    n_seq, seq, _ = um.shape
    L = M_CHUNK
    n_chunks = seq // L
    tok = pl.BlockSpec((1, L, M_WIDTH), lambda b, c: (b, c, 0))
    const = lambda shape: pl.BlockSpec(shape, lambda b, c: (0,) * len(shape))
    per_seq = lambda shape: pl.BlockSpec((1,) + shape, lambda b, c: (b,) + (0,) * len(shape))
    return pl.pallas_call(
        _mlstm_kernel,
        grid=(n_seq, n_chunks),
        in_specs=[tok,
                  pl.BlockSpec((M_WIDTH, L), lambda b, c: (0, b * n_chunks + c)),
                  pl.BlockSpec((1, GATE_ROWS, L), lambda b, c: (b, 0, c)),
                  per_seq((8, M_WIDTH)), per_seq((M_HEADS, M_DIM, M_DIM)),
                  per_seq((M_HEADS, M_DIM, LANES)), per_seq((1, LANES)),
                  const((L, L)), const((CONV_W, M_WIDTH)), const((1, M_WIDTH)),
                  const((M_WIDTH, M_WIDTH)), const((M_WIDTH, M_WIDTH)), const((M_WIDTH, M_WIDTH)),
                  const((1, M_WIDTH)), const((1, M_WIDTH))],
        out_specs=[tok, per_seq((M_HEADS, M_DIM, M_DIM)), per_seq((M_HEADS, M_DIM, LANES)),
                   per_seq((1, LANES))],
        out_shape=[jax.ShapeDtypeStruct((n_seq, seq, M_WIDTH), F32),
                   jax.ShapeDtypeStruct((n_seq, M_HEADS, M_DIM, M_DIM), F32),
                   jax.ShapeDtypeStruct((n_seq, M_HEADS, M_DIM, LANES), F32),
                   jax.ShapeDtypeStruct((n_seq, 1, LANES), F32)],
        scratch_shapes=[pltpu.VMEM((L + 8, M_WIDTH), F32),
                        pltpu.VMEM((M_HEADS, M_DIM, M_DIM), F32),
                        pltpu.VMEM((M_HEADS, M_DIM, LANES), F32),
                        pltpu.VMEM((1, LANES), F32)],
        compiler_params=_cparams(("arbitrary", "arbitrary")),
        name="mlstm",
    )(um, vm_t, gates, hist8, ct0, n0, m0, tri, cw, cb, wk_bd, wq_bd.T, wk_bd.T, mnorm, mskip)


def _lambda(lq1_ref, lk1_ref, lq2_ref, lk2_ref, lam_init):
    e1 = jnp.exp(jnp.sum(lq1_ref[...] * lk1_ref[...], axis=1, keepdims=True))
    e2 = jnp.exp(jnp.sum(lq2_ref[...] * lk2_ref[...], axis=1, keepdims=True))
    return e1 - e2 + lam_init


def _attn_kernel(*refs, tq, tk, rc, fox, lam_init):
    if fox:
        q_ref, k_ref, v_ref, f_ref, o_ref, qs, s_a, s_b, p_a, p_b, al_a, al_b, m_s, l_s, acc = refs
    else:
        (q_ref, k_ref, v_ref, dn_ref, lq1_ref, lk1_ref, lq2_ref, lk2_ref,
         o_ref, qs, s_a, s_b, p_a, p_b, al_a, al_b, m_s, l_s, acc) = refs
    s_buf, p_buf, al_buf = (s_a, s_b), (p_a, p_b), (al_a, al_b)
    i = pl.program_id(2)
    j_last = ((i + 1) * tq - 1) // tk
    half = LANES // 2
    f_row = 12 + 2 * pl.program_id(1)

    q = q_ref[...].astype(F32)
    lane = lax.broadcasted_iota(jnp.int32, q.shape, 1)
    qs[0:tq, :] = jnp.where(lane < half, q, 0.0).astype(BF16)
    qs[tq:2 * tq, :] = jnp.where(lane >= half, q, 0.0).astype(BF16)
    m_s[...] = jnp.full_like(m_s, NEG_INF)
    l_s[...] = jnp.zeros_like(l_s)
    acc[...] = jnp.zeros_like(acc)
    p_b[...] = jnp.zeros_like(p_b)
    al_b[...] = jnp.ones_like(al_b)

    diag = (lax.broadcasted_iota(jnp.int32, (rc, tk), 1) - lax.broadcasted_iota(jnp.int32, (rc, tk), 0))

    def scores(t, slot):
        k0 = pl.multiple_of(t * tk, tk)
        s_buf[slot][...] = _dot_nt(qs[...], k_ref[pl.ds(k0, tk), :])

    def values(t, slot):
        k0 = pl.multiple_of(jnp.maximum(t, 0) * tk, tk)
        acc[...] = al_buf[slot][...] * acc[...] + _dot(p_buf[slot][...], v_ref[pl.ds(k0, tk), :])

    def softmax(t, slot, masked):
        k0 = pl.multiple_of(t * tk, tk)
        if fox:
            f_top = f_ref[pl.ds(f_row, 1), pl.ds(k0, tk)]
            f_bot = f_ref[pl.ds(f_row + 1, 1), pl.ds(k0, tk)]
        for c in range(2 * tq // rc):
            rows = slice(c * rc, (c + 1) * rc)
            s = s_buf[slot][rows, :]
            if fox:
                s = s - (f_top if c * rc < tq else f_bot)
            if masked:
                s = jnp.where(diag <= i * tq + (c * rc) % tq - k0, s, NEG_INF)
            m_prev = m_s[rows, :]
            m_new = jnp.maximum(m_prev, jnp.max(s, axis=1, keepdims=True))
            alpha = jnp.exp(m_prev - m_new)
            p = jnp.exp(s - jnp.concatenate([m_new] * (tk // LANES), axis=1))
            l_s[rows, :] = alpha * l_s[rows, :] + jnp.sum(p, axis=1, keepdims=True)
            m_s[rows, :] = m_new
            al_buf[slot][rows, :] = alpha
            p_buf[slot][rows, :] = p.astype(BF16)

    def step(t, slot):
        scores(t + 1, 1 - slot)
        values(t - 1, 1 - slot)
        softmax(t, slot, False)

    def last(slot):
        values(j_last - 1, 1 - slot)
        softmax(j_last, slot, True)
        values(j_last, slot)

    scores(0, 0)

    def pair(u, carry):
        step(2 * u, 0)
        step(2 * u + 1, 1)
        return carry

    lax.fori_loop(0, j_last // 2, pair, 0)

    @pl.when(j_last % 2 == 1)
    def _():
        step(j_last - 1, 0)
        last(1)

    @pl.when(j_last % 2 == 0)
    def _():
        last(0)

    o = acc[...] / l_s[...]
    top, bot = o[0:tq, :], o[tq:2 * tq, :]
    if fox:
        o_ref[...] = jnp.where(lane < half, top, bot)
    else:
        lam = _lambda(lq1_ref, lk1_ref, lq2_ref, lk2_ref, lam_init)
        od = top - lam * bot
        od = od * lax.rsqrt(jnp.mean(od * od, axis=1, keepdims=True) + EPS)
        o_ref[...] = od * dn_ref[...] * (1.0 - lam_init)


def _prompt_attn(q, k, v, extra, n_seq, seq, tq, tk, rc, fox, lam_init):
    n, width = q.shape
    groups = width // LANES
    nq = seq // tq
    q_spec = pl.BlockSpec((tq, LANES), lambda b, h, i: (b * nq + i, h))
    kv_spec = pl.BlockSpec((seq, LANES), lambda b, h, i: (b, h))
    if fox:
        extra_specs = [pl.BlockSpec((GATE_ROWS, seq), lambda b, h, i: (0, b))]
    else:
        lam_spec = pl.BlockSpec((1, D_QK), lambda b, h, i: (0, 0))
        extra_specs = [pl.BlockSpec((1, LANES), lambda b, h, i: (0, h))] + [lam_spec] * 4
    return pl.pallas_call(
        functools.partial(_attn_kernel, tq=tq, tk=tk, rc=rc, fox=fox, lam_init=lam_init),
        grid=(n_seq, groups, nq),
        in_specs=[q_spec, kv_spec, kv_spec] + extra_specs,
        out_specs=q_spec,
        out_shape=jax.ShapeDtypeStruct((n, width), F32),
        scratch_shapes=[pltpu.VMEM((2 * tq, LANES), BF16)]
                       + [pltpu.VMEM((2 * tq, tk), F32)] * 2
                       + [pltpu.VMEM((2 * tq, tk), BF16)] * 2
                       + [pltpu.VMEM((2 * tq, LANES), F32)] * 2
                       + [pltpu.VMEM((2 * tq, LANES), F32)] * 3,
        compiler_params=_cparams(("arbitrary",) * 3),
        name="fox_attn" if fox else "diff_attn",
    )(q, k, v, *extra)


D_ROWS = PAGE_SIZE * D_HEADS


def _rows_from_segments(row_vec, n_seg, r):
    out = jnp.zeros((8, LANES), F32)
    for s in range(n_seg):
        seg = jnp.broadcast_to(row_vec[:, s * LANES:(s + 1) * LANES], (8, LANES))
        out = jnp.where(lax.shift_right_logical(r, 1) == s, seg, out)
    return out


def _decode_kernel(pt_ref, *refs, pages_per_step, lam_init):
    G = pages_per_step
    (dq_ref, fq_ref, dkn_ref, dvn_ref, fkn_ref, fvn_ref, gn_ref,
     dn_ref, lq1_ref, lk1_ref, lq2_ref, lk2_ref, tri_ref) = refs[:13]
    page_refs = refs[13:13 + 5 * G]
    od_ref, of_ref = refs[13 + 5 * G:15 + 5 * G]
    qd_s, qf_s, md_s, ld_s, accd, mf_s, lf_s, accf, fcar = refs[15 + 5 * G:]
    j = pl.program_id(1)
    r8 = lax.broadcasted_iota(jnp.int32, (8, LANES), 0)
    l8 = lax.broadcasted_iota(jnp.int32, (8, LANES), 1)
    half_sel = lax.shift_right_logical(l8, 6) == (r8 & 1)

    def head_cols(shape):
        r = lax.broadcasted_iota(jnp.int32, shape, 0)
        c = lax.broadcasted_iota(jnp.int32, shape, 1)
        return lax.shift_right_logical(c, 6) == r

    @pl.when(j == 0)
    def _():
        qd = _rows_from_segments(dq_ref[0].astype(F32), D_HEADS, r8)
        qd_s[...] = jnp.where(half_sel, qd, 0.0).astype(BF16)
        qf = jnp.broadcast_to(fq_ref[0].astype(F32), (8, F_WIDTH))
        qf_s[...] = jnp.where(head_cols(qf.shape), qf, 0.0).astype(BF16)
        md_s[...] = jnp.full_like(md_s, NEG_INF)
        mf_s[...] = jnp.full_like(mf_s, NEG_INF)
        ld_s[...] = jnp.zeros_like(ld_s)
        lf_s[...] = jnp.zeros_like(lf_s)
        accd[...] = jnp.zeros_like(accd)
        accf[...] = jnp.zeros_like(accf)
        fcar[...] = jnp.zeros_like(fcar)

    def online(s, v_refs, width, pv_dot, m_ref, l_ref, acc_ref):
        m_prev = m_ref[...]
        m_new = jnp.maximum(m_prev, jnp.max(s, axis=1, keepdims=True))
        alpha = jnp.exp(m_prev - m_new)
        p = jnp.exp(s - m_new)
        l_ref[...] = alpha * l_ref[...] + jnp.sum(p, axis=1, keepdims=True)
        pb = p.astype(BF16)
        pv = pv_dot(pb[:, 0:width], v_refs[0][0, 0].astype(BF16))
        for g in range(1, G):
            pv = pv + pv_dot(pb[:, g * width:(g + 1) * width], v_refs[g][0, 0].astype(BF16))
        acc_ref[...] = alpha * acc_ref[...] + pv
        m_ref[...] = m_new

    kd_refs = page_refs[0::5]
    vd_refs = page_refs[1::5]
    kf_refs = page_refs[2::5]
    vf_refs = page_refs[3::5]
    fl_refs = page_refs[4::5]

    qd = qd_s[...]
    s_d = jnp.concatenate([_dot_nt(qd, kd_refs[g][0, 0].astype(BF16)) for g in range(G)], axis=1)
    rd = lax.broadcasted_iota(jnp.int32, s_d.shape, 0)
    cd = lax.broadcasted_iota(jnp.int32, s_d.shape, 1)
    s_d = jnp.where((cd & (D_HEADS - 1)) == lax.shift_right_logical(rd, 1), s_d, NEG_INF)
    online(s_d, vd_refs, D_ROWS, _dot, md_s, ld_s, accd)

    qf = qf_s[...]
    tri = tri_ref[...]
    carry = fcar[...]
    parts = []
    for g in range(G):
        fl = fl_refs[g][0, 0]
        hi, mid, lo = _split3(fl)
        cum = _dot(hi, tri) + _dot(mid, tri) + _dot(lo, tri) + carry
        carry = carry + jnp.sum(fl, axis=1, keepdims=True)
        parts.append(_dot(qf, kf_refs[g][0, 0].astype(BF16)) - cum)
    fcar[...] = carry
    online(jnp.concatenate(parts, axis=1), vf_refs, PAGE_SIZE, _dot_nt, mf_s, lf_s, accf)

    @pl.when(j == pl.num_programs(1) - 1)
    def _():
        def add_new(q_s, k_new, v_new, bias, m_ref, l_ref, acc_ref):
            s = jnp.sum(q_s[...].astype(F32) * k_new, axis=1, keepdims=True) - bias
            m_prev = m_ref[...]
            m_new = jnp.maximum(m_prev, s)
            alpha = jnp.exp(m_prev - m_new)
            p = jnp.exp(s - m_new)
            l_new = alpha * l_ref[...] + p
            return (alpha * acc_ref[...] + p * v_new) / l_new

        od_all = add_new(qd_s, _rows_from_segments(dkn_ref[0], D_HEADS, r8),
                         _rows_from_segments(dvn_ref[0], D_HEADS, r8), 0.0, md_s, ld_s, accd)
        f_new = fcar[...] + gn_ref[0]
        of_all = add_new(qf_s, fkn_ref[0], fvn_ref[0], f_new, mf_s, lf_s, accf)

        lam = _lambda(lq1_ref, lk1_ref, lq2_ref, lk2_ref, lam_init)
        parts_d = []
        for h in range(D_HEADS):
            seg = od_all[2 * h:2 * h + 1, :] - lam * od_all[2 * h + 1:2 * h + 2, :]
            parts_d.append(seg * lax.rsqrt(jnp.mean(seg * seg, axis=1, keepdims=True) + EPS))
        od_ref[0] = jnp.concatenate(parts_d, axis=1) * dn_ref[...] * (1.0 - lam_init)
        of_ref[0] = jnp.sum(jnp.where(head_cols(of_all.shape), of_all, 0.0), axis=0, keepdims=True)


def _decode_attn(page_table, layer, dq, fq, dk_new, dv_new, fk_new, fv_new, g_new,
                 d_norm, lq1, lk1, lq2, lk2, tri, c_dk, c_dv, c_fk, c_fv, c_flt,
                 pages_per_step, lam_init):
    nb, n_pages = page_table.shape
    G = pages_per_step
    steps = n_pages // G
    per_b = lambda w: pl.BlockSpec((1, 1, w), lambda b, j, pt: (b, 0, 0))
    const = lambda shape: pl.BlockSpec(shape, lambda b, j, pt: (0,) * len(shape))
    in_specs = [per_b(D_WIDTH), per_b(F_WIDTH), per_b(D_WIDTH), per_b(D_WIDTH),
                per_b(F_WIDTH), per_b(F_WIDTH),
                pl.BlockSpec((1, 8, 1), lambda b, j, pt: (b, 0, 0)),
                const((1, D_WIDTH))] + [const((1, D_QK))] * 4 + [const((PAGE_SIZE, PAGE_SIZE))]
    args = [dq, fq, dk_new, dv_new, fk_new, fv_new, g_new, d_norm, lq1, lk1, lq2, lk2, tri]
    for g in range(G):
        def page(rows, w, g=g):
            return pl.BlockSpec((1, 1, rows, w),
                                lambda b, j, pt: (layer, pt[b, j * G + g], 0, 0))
        in_specs += [page(D_ROWS, LANES), page(D_ROWS, LANES),
                     page(F_WIDTH, PAGE_SIZE), page(F_WIDTH, PAGE_SIZE), page(8, PAGE_SIZE)]
        args += [c_dk, c_dv, c_fk, c_fv, c_flt]
    grid_spec = pltpu.PrefetchScalarGridSpec(
        num_scalar_prefetch=1,
        grid=(nb, steps),
        in_specs=in_specs,
        out_specs=[per_b(D_WIDTH), per_b(F_WIDTH)],
        scratch_shapes=[pltpu.VMEM((8, LANES), BF16), pltpu.VMEM((8, F_WIDTH), BF16),
                        pltpu.VMEM((8, 1), F32), pltpu.VMEM((8, 1), F32),
                        pltpu.VMEM((8, LANES), F32),
                        pltpu.VMEM((8, 1), F32), pltpu.VMEM((8, 1), F32),
                        pltpu.VMEM((8, F_WIDTH), F32),
                        pltpu.VMEM((8, 1), F32)])
    return pl.pallas_call(
        functools.partial(_decode_kernel, pages_per_step=G, lam_init=lam_init),
        grid_spec=grid_spec,
        out_shape=[jax.ShapeDtypeStruct((nb, 1, D_WIDTH), F32),
                   jax.ShapeDtypeStruct((nb, 1, F_WIDTH), F32)],
        compiler_params=_cparams(("arbitrary", "arbitrary")),
        name="decode_attn",
    )(page_table, *args)


def _out_proj_kernel(x_ref, hm_ref, od_ref, of_ref, z_ref, w_ref, g_ref, o_ref):
    h = jnp.concatenate([hm_ref[...], od_ref[...], of_ref[...]], axis=1) * _silu(z_ref[...])
    y = _dot(h.astype(BF16), w_ref[...])
    yn = y * lax.rsqrt(jnp.mean(y * y, axis=-1, keepdims=True) + EPS) * g_ref[...]
    o_ref[...] = x_ref[...] + yn


def _out_proj(x2d, hm, od, of, z, w_out, norm_g, tm):
    n = x2d.shape[0]
    row_spec = lambda w: pl.BlockSpec((tm, w), lambda i: (i, 0))
    const = lambda shape: pl.BlockSpec(shape, lambda i: (0,) * len(shape))
    return pl.pallas_call(
        _out_proj_kernel,
        grid=(n // tm,),
        in_specs=[row_spec(D_MODEL), row_spec(M_WIDTH), row_spec(D_WIDTH), row_spec(F_WIDTH),
                  row_spec(MIX_WIDTH), const((MIX_WIDTH, D_MODEL)), const((1, D_MODEL))],
        out_specs=row_spec(D_MODEL),
        out_shape=jax.ShapeDtypeStruct((n, D_MODEL), F32),
        compiler_params=_cparams(("arbitrary",)),
        name="out_proj",
    )(x2d, hm, od, of, z, w_out, norm_g)


def _block_diag(w):
    hh, d, _ = w.shape
    out = jnp.zeros((hh * d, hh * d), w.dtype)
    for h in range(hh):
        out = out.at[h * d:(h + 1) * d, h * d:(h + 1) * d].set(w[h])
    return out


def _layer_weights(l, w_in, b_i, b_f, b_fox, wq_m, wk_m):
    idx = [0]
    for s in SPLITS:
        idx.append(idx[-1] + s)
    cols = lambda k: w_in[l][:, idx[k]:idx[k + 1]]
    u_m, v_m, ig, fg, dq, dk, dv, fq, fk, fv, ffg, z = [cols(k) for k in range(len(SPLITS))]
    w_main = jnp.concatenate([u_m, v_m, dq * (D_QK ** -0.5), dk, dv, fq * (F_DIM ** -0.5), fk, fv, z],
                             axis=1).astype(BF16)
    w_gate_t = jnp.concatenate([ig, fg, ffg, jnp.zeros((D_MODEL, GATE_ROWS - 12), F32)],
                               axis=1).T.astype(BF16)
    gate_bias = jnp.concatenate([b_i[l], b_f[l], b_fox[l], jnp.zeros((GATE_ROWS - 12,), F32)]
                                ).astype(F32).reshape(GATE_ROWS, 1)
    wq_bd = _block_diag(wq_m[l]).astype(BF16)
    wk_bd = (_block_diag(wk_m[l]) * (M_DIM ** -0.5)).astype(BF16)
    return w_main, v_m.T.astype(BF16), w_gate_t, gate_bias, wq_bd, wk_bd


def kernel(x_prompt, x_sample, cache_dk, cache_dv, cache_fk, cache_fv, cache_flogf, state_mconv, state_mC, state_mn, state_mm, page_table, norm_pre, norm_post, w_in, conv_w, conv_b, wq_m, wk_m, b_i, b_f, m_norm, m_skip, lam_q1, lam_k1, lam_q2, lam_k2, d_norm, b_fox, w_out):
    bp, seq, _ = x_prompt.shape
    nb = x_sample.shape[0]
    n_pool = cache_dk.shape[1]
    n_p = bp * seq
    tm_p = 512
    tq, tk = 256, 512
    rc = 64
    pages_per_step = 16
    L = M_CHUNK

    xp = x_prompt.reshape(n_p, D_MODEL)
    ns = LANES
    xs = jnp.pad(x_sample.reshape(nb, D_MODEL), ((0, ns - nb), (0, 0)))
    c_dk = cache_dk.reshape(DEPTH, n_pool, D_ROWS, LANES)
    c_dv = cache_dv.reshape(DEPTH, n_pool, D_ROWS, LANES)
    c_fk = jnp.transpose(cache_fk, (0, 1, 3, 4, 2)).reshape(DEPTH, n_pool, F_WIDTH, PAGE_SIZE)
    c_fv = jnp.transpose(cache_fv, (0, 1, 3, 4, 2)).reshape(DEPTH, n_pool, F_WIDTH, PAGE_SIZE)
    c_flt = jnp.pad(jnp.swapaxes(cache_flogf.astype(F32), 2, 3), ((0, 0), (0, 0), (0, 8 - F_HEADS), (0, 0)))

    tri_p = (lax.broadcasted_iota(jnp.int32, (tm_p, tm_p), 0)
             <= lax.broadcasted_iota(jnp.int32, (tm_p, tm_p), 1)).astype(BF16)
    tri_s = jnp.eye(ns, dtype=BF16)
    tri_page = tri_p[:PAGE_SIZE, :PAGE_SIZE]

    zero_hist = jnp.zeros((bp, 8, M_WIDTH), F32)
    zero_c = jnp.zeros((bp, M_HEADS, M_DIM, M_DIM), F32)
    zero_n = jnp.zeros((bp, M_HEADS, M_DIM, LANES), F32)
    zero_m = jnp.zeros((bp, 1, LANES), F32)

    new_p, new_s = [], []
    for l in range(DEPTH):
        lam_init = 0.8 - 0.6 * math.exp(-0.3 * l)
        w_main, w_vm_t, w_gate_t, gate_bias, wq_bd, wk_bd = _layer_weights(l, w_in, b_i, b_f, b_fox, wq_m, wk_m)
        w_out_l = w_out[l].astype(BF16)
        g_pre = norm_pre[l].reshape(1, D_MODEL)
        g_post = norm_post[l].reshape(1, D_MODEL)
        cw, cb = conv_w[l], conv_b[l].reshape(1, M_WIDTH)
        mnorm, mskip = m_norm[l].reshape(1, M_WIDTH), m_skip[l].reshape(1, M_WIDTH)
        dn = d_norm[l].reshape(1, D_WIDTH)
        lams = [a[l].reshape(1, D_QK).astype(F32) for a in (lam_q1, lam_k1, lam_q2, lam_k2)]

        (um, vm, dq, dk32, dk16, dv32, dv16, fq, fk32, fk16, fv32, fv16, z, gates) = _in_proj(
            xp, g_pre, w_main, w_vm_t, w_gate_t, gate_bias, tri_p, tm_p, seq // tm_p)
        gates_seq = gates.reshape(GATE_ROWS, bp, seq).transpose(1, 0, 2)
        hm, ct_fin, n_fin, m_fin = _mlstm(um.reshape(bp, seq, M_WIDTH), vm, gates_seq,
                                          zero_hist, zero_c, zero_n, zero_m, tri_page,
                                          cw, cb, wq_bd, wk_bd, mnorm, mskip)
        c_fin, n_fin = jnp.swapaxes(ct_fin, 2, 3), n_fin[..., 0]
        hm = hm.reshape(n_p, M_WIDTH)
        od = _prompt_attn(dq, dk16, dv16, [dn] + lams, bp, seq, tq, tk, rc, False, lam_init)
        of = _prompt_attn(fq, fk16, fv16, [gates], bp, seq, tq, tk, rc, True, lam_init)
        xp = _out_proj(xp, hm, od, of, z, w_out_l, g_post, tm_p)
        new_p.append((dk32.reshape(bp, seq, D_HEADS, 2 * D_QK), dv32.reshape(bp, seq, D_HEADS, D_V),
                      fk32.reshape(bp, seq, F_HEADS, F_DIM), fv32.reshape(bp, seq, F_HEADS, F_DIM),
                      gates[8:12].T.reshape(bp, seq, F_HEADS),
                      um.reshape(bp, seq, M_WIDTH)[:, seq - (CONV_W - 1):],
                      c_fin, n_fin, m_fin[:, 0, :M_HEADS]))

        outs_s = _in_proj(xs, g_pre, w_main, w_vm_t, w_gate_t, gate_bias, tri_s, ns, 1)
        (um, vm, dq, dk32, _, dv32, _, fq, fk32, _, fv32, _, z, gates) = [
            a[:, :nb] if k in (1, 13) else a[:nb * (a.shape[0] // ns)] for k, a in enumerate(outs_s)]
        z = outs_s[12]
        dk32, dv32 = dk32.reshape(nb, D_WIDTH), dv32.reshape(nb, D_WIDTH)
        pad_tok = lambda a: jnp.pad(a[:, None, :], ((0, 0), (0, L - 1), (0, 0)))
        noop = jnp.where(jnp.arange(GATE_ROWS) < M_HEADS, NEG_INF, 0.0).astype(F32)
        gates_pad = jnp.concatenate(
            [gates.T[:, :, None], jnp.broadcast_to(noop[None, :, None], (nb, GATE_ROWS, L - 1))], axis=2)
        hist = state_mconv[l].astype(F32)
        hist8 = jnp.pad(hist, ((0, 0), (8 - (CONV_W - 1), 0), (0, 0)))
        m0 = jnp.pad(state_mm[l].astype(F32), ((0, 0), (0, LANES - M_HEADS)))[:, None, :]
        vm_pad = jnp.pad(vm[:, :, None], ((0, 0), (0, 0), (0, L - 1))).reshape(M_WIDTH, nb * L)
        n0 = jnp.broadcast_to(state_mn[l].astype(F32)[..., None], (nb, M_HEADS, M_DIM, LANES))
        hm, ct_fin, n_fin, m_fin = _mlstm(pad_tok(um), vm_pad, gates_pad, hist8,
                                          jnp.swapaxes(state_mC[l].astype(F32), 2, 3), n0, m0, tri_page,
                                          cw, cb, wq_bd, wk_bd, mnorm, mskip)
        c_fin, n_fin = jnp.swapaxes(ct_fin, 2, 3), n_fin[..., 0]
        hm = hm[:, 0]
        flog_new = gates[8:12].T
        g_new = jnp.pad(flog_new, ((0, 0), (0, 8 - F_HEADS)))[:, :, None]
        od, of = _decode_attn(page_table, l, dq[:, None, :], fq[:, None, :], dk32[:, None, :],
                              dv32[:, None, :], fk32[:, None, :], fv32[:, None, :], g_new,
                              dn, *lams, tri_page, c_dk, c_dv, c_fk, c_fv, c_flt, pages_per_step, lam_init)
        pad_rows = lambda a: jnp.pad(a, ((0, ns - nb), (0, 0)))
        xs = _out_proj(xs, pad_rows(hm), pad_rows(od[:, 0]), pad_rows(of[:, 0]), z, w_out_l, g_post, ns)
        new_s.append((dk32.reshape(nb, 1, D_HEADS, 2 * D_QK), dv32.reshape(nb, 1, D_HEADS, D_V),
                      fk32.reshape(nb, 1, F_HEADS, F_DIM), fv32.reshape(nb, 1, F_HEADS, F_DIM),
                      flog_new.reshape(nb, 1, F_HEADS),
                      jnp.concatenate([hist[:, 1:], um[:, None, :]], axis=1),
                      c_fin, n_fin, m_fin[:, 0, :M_HEADS]))

    stack = lambda states, k: jnp.stack([st[k] for st in states])
    return ((xp.reshape(bp, seq, D_MODEL), xs[:nb].reshape(nb, 1, D_MODEL))
            + tuple(stack(new_p, k) for k in range(9))
            + tuple(stack(new_s, k) for k in range(9)))
```

```python
import functools
import math

import jax
import jax.numpy as jnp
from jax import lax
from jax.experimental import pallas as pl
from jax.experimental.pallas import tpu as pltpu

D_MODEL = 1024
DEPTH = 4
PAGE_SIZE = 128
M_HEADS = 4
M_DIM = 64
M_WIDTH = M_HEADS * M_DIM
CONV_W = 4
M_CHUNK = 128
D_HEADS = 4
D_QK = 64
D_V = 2 * D_QK
D_WIDTH = D_HEADS * D_V
F_HEADS = 4
F_DIM = 64
F_WIDTH = F_HEADS * F_DIM
MIX_WIDTH = M_WIDTH + D_WIDTH + F_WIDTH
EPS = 1e-6
SPLITS = (M_WIDTH, M_WIDTH, M_HEADS, M_HEADS, D_WIDTH, D_WIDTH, D_WIDTH,
          F_WIDTH, F_WIDTH, F_WIDTH, F_HEADS, MIX_WIDTH)

LANES = 128
GATE_ROWS = 16
VMEM_LIMIT = 56 * 1024 * 1024

F32 = jnp.float32
BF16 = jnp.bfloat16
NEG_INF = float("-inf")

_MAIN = (("um", M_WIDTH), ("vm", M_WIDTH), ("dq", D_WIDTH), ("dk", D_WIDTH), ("dv", D_WIDTH),
         ("fq", F_WIDTH), ("fk", F_WIDTH), ("fv", F_WIDTH), ("z", MIX_WIDTH))
_MAIN_OFF = {}
_o = 0
for _n, _w in _MAIN:
    _MAIN_OFF[_n] = (_o, _w)
    _o += _w
MAIN_WIDTH = _o


def _cparams(sem):
    return pltpu.CompilerParams(dimension_semantics=sem, vmem_limit_bytes=VMEM_LIMIT)


def _log_sigmoid(x):
    return jnp.minimum(x, 0.0) - jnp.log1p(jnp.exp(-jnp.abs(x)))


def _silu(x):
    return x * (1.0 / (1.0 + jnp.exp(-x)))


def _dot(a, b):
    return jnp.dot(a, b, preferred_element_type=F32)


def _dot_nt(a, b):
    return lax.dot_general(a, b, (((1,), (1,)), ((), ())), preferred_element_type=F32)


def _split3(x):
    hi = x.astype(BF16)
    r1 = x - hi.astype(F32)
    mid = r1.astype(BF16)
    lo = (r1 - mid.astype(F32)).astype(BF16)
    return hi, mid, lo


def _in_proj_kernel(x_ref, g_ref, w_ref, wvt_ref, wg_ref, gb_ref, tri_ref,
                    um_ref, vmt_ref, dq_ref, dk32_ref, dk16_ref, dv32_ref, dv16_ref,
                    fq_ref, fk32_ref, fk16_ref, fv32_ref, fv16_ref, z_ref, gates_ref,
                    carry_ref, *, tiles_per_seq):
    i = pl.program_id(0)
    x = x_ref[...]
    xn = x * lax.rsqrt(jnp.mean(x * x, axis=-1, keepdims=True) + EPS) * g_ref[...]
    xb = xn.astype(BF16)

    def proj(name):
        off, width = _MAIN_OFF[name]
        return _dot(xb, w_ref[:, off:off + width])

    um_ref[...] = proj("um")
    vmt_ref[...] = _dot_nt(wvt_ref[...], xb)
    dq_ref[...] = proj("dq").astype(BF16)
    dk = proj("dk")
    tm = x.shape[0]
    for h in range(D_HEADS):
        dk32_ref[pl.ds(h, tm, stride=D_HEADS), :] = dk[:, h * D_V:(h + 1) * D_V]
    dk16_ref[...] = dk.astype(BF16)
    dv = proj("dv")
    for h in range(D_HEADS):
        dv32_ref[pl.ds(h, tm, stride=D_HEADS), :] = dv[:, h * D_V:(h + 1) * D_V]
    dv16_ref[...] = dv.astype(BF16)
    fq_ref[...] = proj("fq").astype(BF16)
    fk = proj("fk")
    fk32_ref[...] = fk
    fk16_ref[...] = fk.astype(BF16)
    fv = proj("fv")
    fv32_ref[...] = fv
    fv16_ref[...] = fv.astype(BF16)
    z_ref[...] = proj("z")

    pre = _dot_nt(wg_ref[...], xb) + gb_ref[...]
    row = lax.broadcasted_iota(jnp.int32, pre.shape, 0)
    act = jnp.where(row < M_HEADS, pre, _log_sigmoid(pre))

    @pl.when(i % tiles_per_seq == 0)
    def _():
        carry_ref[...] = jnp.zeros_like(carry_ref)

    hi, mid, lo = _split3(act)
    tri = tri_ref[...]
    cs = _dot(hi, tri) + _dot(mid, tri) + _dot(lo, tri) + carry_ref[:, 0:1]
    carry_ref[...] = jnp.broadcast_to(cs[:, cs.shape[1] - 1:], carry_ref.shape)
    shifted = pltpu.roll(cs, 4, axis=0)
    gates_ref[...] = jnp.where((row >= 12), shifted, jnp.where(row < 12, act, 0.0))


def _in_proj(x2d, norm_g, w_main, w_vm_t, w_gate_t, gate_bias, tri, tm, tiles_per_seq):
    n = x2d.shape[0]
    grid = (n // tm,)
    row_spec = lambda w: pl.BlockSpec((tm, w), lambda i: (i, 0))
    const = lambda shape: pl.BlockSpec(shape, lambda i: (0,) * len(shape))
    out_shape = [
        jax.ShapeDtypeStruct((n, M_WIDTH), F32),
        jax.ShapeDtypeStruct((M_WIDTH, n), F32),
        jax.ShapeDtypeStruct((n, D_WIDTH), BF16),
        jax.ShapeDtypeStruct((n * D_HEADS, D_V), F32),
        jax.ShapeDtypeStruct((n, D_WIDTH), BF16),
        jax.ShapeDtypeStruct((n * D_HEADS, D_V), F32),
        jax.ShapeDtypeStruct((n, D_WIDTH), BF16),
        jax.ShapeDtypeStruct((n, F_WIDTH), BF16),
        jax.ShapeDtypeStruct((n, F_WIDTH), F32),
        jax.ShapeDtypeStruct((n, F_WIDTH), BF16),
        jax.ShapeDtypeStruct((n, F_WIDTH), F32),
        jax.ShapeDtypeStruct((n, F_WIDTH), BF16),
        jax.ShapeDtypeStruct((n, MIX_WIDTH), F32),
        jax.ShapeDtypeStruct((GATE_ROWS, n), F32),
    ]
    out_specs = [pl.BlockSpec((tm * (s.shape[0] // n), s.shape[1]), lambda i: (i, 0)) for s in out_shape[:-1]]
    out_specs[1] = pl.BlockSpec((M_WIDTH, tm), lambda i: (0, i))
    out_specs.append(pl.BlockSpec((GATE_ROWS, tm), lambda i: (0, i)))
    return pl.pallas_call(
        functools.partial(_in_proj_kernel, tiles_per_seq=tiles_per_seq),
        grid=grid,
        in_specs=[row_spec(D_MODEL), const((1, D_MODEL)), const((D_MODEL, MAIN_WIDTH)),
                  const((M_WIDTH, D_MODEL)), const((GATE_ROWS, D_MODEL)), const((GATE_ROWS, 1)),
                  const((tm, tm))],
        out_specs=out_specs,
        out_shape=out_shape,
        scratch_shapes=[pltpu.VMEM((GATE_ROWS, LANES), F32)],
        compiler_params=_cparams(("arbitrary",)),
        name="in_proj",
    )(x2d, norm_g, w_main, w_vm_t, w_gate_t, gate_bias, tri)


def _mlstm_kernel(um_ref, vt_ref, gates_ref, hist_ref, ct0_ref, n0_ref, m0_ref,
                  tri_ref, cw_ref, cb_ref, wk_ref, wqt_ref, wkt_ref, mnorm_ref, mskip_ref,
                  hm_ref, ct_out_ref, n_out_ref, m_out_ref,
                  ubuf, ct_s, n_s, m_s):
    c = pl.program_id(1)
    L = M_CHUNK

    @pl.when(c == 0)
    def _():
        ubuf[0:8, :] = hist_ref[0]
        ct_s[...] = ct0_ref[0]
        n_s[...] = n0_ref[0]
        m_s[...] = m0_ref[0]

    ubuf[8:8 + L, :] = um_ref[0]
    conv = cb_ref[...]
    for j in range(CONV_W):
        conv = conv + ubuf[5 + j:5 + j + L, :] * cw_ref[j:j + 1, :]
    ubuf[0:8, :] = ubuf[L:L + 8, :]
    xc = _silu(conv)
    xb = xc.astype(BF16)
    k_all = _dot(xb, wk_ref[...]).astype(BF16)
    qt_all = _dot_nt(wqt_ref[...], xb)
    kt_all = _dot_nt(wkt_ref[...], xb)
    vt_all = vt_ref[...]

    gates = gates_ref[0][0:8, :]
    row8 = lax.broadcasted_iota(jnp.int32, gates.shape, 0)
    hi, mid, lo = _split3(jnp.where(row8 >= M_HEADS, gates, 0.0))
    tri = tri_ref[...]
    bcum = _dot(hi, tri) + _dot(mid, tri) + _dot(lo, tri)
    a_rows = gates - pltpu.roll(bcum, M_HEADS, axis=0)

    si = lax.broadcasted_iota(jnp.int32, (L, L), 0)
    ti = lax.broadcasted_iota(jnp.int32, (L, L), 1)
    causal = si <= ti
    eye = si == ti
    zeros_half = jnp.zeros((M_DIM, L), BF16)

    m_prev_all = m_s[...]
    m_new_all = m_prev_all
    lane1 = lax.broadcasted_iota(jnp.int32, m_prev_all.shape, 1)
    ct_prev_all = [ct_s[h] for h in range(M_HEADS)]
    n_prev_all = [n_s[h] for h in range(M_HEADS)]
    ct_new, n_new, hts = [], [], []
    for h in range(M_HEADS):
        rows = slice(h * M_DIM, (h + 1) * M_DIM)
        pair = slice((h // 2) * LANES, (h // 2 + 1) * LANES)
        qt_h = qt_all[rows, :]
        qt_b = qt_h.astype(BF16)
        kt_b = kt_all[rows, :].astype(BF16)
        vt_b = vt_all[rows, :].astype(BF16)
        m_prev = m_prev_all[:, h:h + 1]
        ct_prev = ct_prev_all[h]
        n_prev = n_prev_all[h]

        a_row = a_rows[h:h + 1, :]
        b_row = bcum[M_HEADS + h:M_HEADS + h + 1, :]
        a_col = jnp.sum(jnp.where(eye, a_row, 0.0), axis=1, keepdims=True)
        g_row = jnp.maximum(jnp.max(jnp.where(causal, a_col, NEG_INF), axis=0, keepdims=True), m_prev)
        m_row = b_row + g_row
        w_inter = jnp.exp(m_prev - g_row)
        wt = jnp.exp(jnp.where(causal, a_col - g_row, NEG_INF))
        qt_pad = jnp.concatenate([qt_b, zeros_half] if h % 2 == 0 else [zeros_half, qt_b], axis=0)
        st = _dot(k_all[:, pair], qt_pad) * wt
        num_t = _dot(vt_b, st.astype(BF16)) + _dot(ct_prev.astype(BF16), qt_b) * w_inter
        qn = jnp.sum(qt_h * n_prev, axis=0, keepdims=True)
        den = w_inter * qn + jnp.sum(st, axis=0, keepdims=True)
        ht = num_t / jnp.maximum(jnp.abs(den), jnp.exp(-m_row))
        hts.append(ht * lax.rsqrt(jnp.mean(ht * ht, axis=0, keepdims=True) + EPS))

        g_last = g_row[:, L - 1:L]
        decay = jnp.exp(m_prev - g_last)
        w_last = jnp.exp(a_row - g_last)
        vtw = (vt_all[rows, :] * w_last).astype(BF16)
        ct_new.append(decay * ct_prev + _dot_nt(vtw, kt_b))
        n_new.append(decay * n_prev + jnp.sum(kt_all[rows, :] * w_last, axis=1, keepdims=True))
        m_new_all = jnp.where(lane1 == h, m_row[:, L - 1:L], m_new_all)
    for h in range(M_HEADS):
        ct_s[h] = ct_new[h]
        n_s[h] = n_new[h]
    m_s[...] = m_new_all
    hn = jnp.concatenate(hts, axis=0).T
    hm_ref[0] = hn * mnorm_ref[...] + mskip_ref[...] * xc

    @pl.when(c == pl.num_programs(1) - 1)
    def _():
        ct_out_ref[0] = ct_s[...]
        n_out_ref[0] = n_s[...]
        m_out_ref[0] = m_s[...]


def _mlstm(um, vm_t, gates, hist8, ct0, n0, m0, tri, cw, cb, wq_bd, wk_bd, mnorm, mskip):
    n_seq, seq, _ = um.shape
    L = M_CHUNK
    n_chunks = seq // L
    tok = pl.BlockSpec((1, L, M_WIDTH), lambda b, c: (b, c, 0))
    const = lambda shape: pl.BlockSpec(shape, lambda b, c: (0,) * len(shape))
    per_seq = lambda shape: pl.BlockSpec((1,) + shape, lambda b, c: (b,) + (0,) * len(shape))
    return pl.pallas_call(
        _mlstm_kernel,
        grid=(n_seq, n_chunks),
        in_specs=[tok,
                  pl.BlockSpec((M_WIDTH, L), lambda b, c: (0, b * n_chunks + c)),
                  pl.BlockSpec((1, GATE_ROWS, L), lambda b, c: (b, 0, c)),
                  per_seq((8, M_WIDTH)), per_seq((M_HEADS, M_DIM, M_DIM)),
                  per_seq((M_HEADS, M_DIM, LANES)), per_seq((1, LANES)),
                  const((L, L)), const((CONV_W, M_WIDTH)), const((1, M_WIDTH)),
                  const((M_WIDTH, M_WIDTH)), const((M_WIDTH, M_WIDTH)), const((M_WIDTH, M_WIDTH)),
                  const((1, M_WIDTH)), const((1, M_WIDTH))],
        out_specs=[tok, per_seq((M_HEADS, M_DIM, M_DIM)), per_seq((M_HEADS, M_DIM, LANES)),
                   per_seq((1, LANES))],
        out_shape=[jax.ShapeDtypeStruct((n_seq, seq, M_WIDTH), F32),
                   jax.ShapeDtypeStruct((n_seq, M_HEADS, M_DIM, M_DIM), F32),
                   jax.ShapeDtypeStruct((n_seq, M_HEADS, M_DIM, LANES), F32),
                   jax.ShapeDtypeStruct((n_seq, 1, LANES), F32)],
        scratch_shapes=[pltpu.VMEM((L + 8, M_WIDTH), F32),
                        pltpu.VMEM((M_HEADS, M_DIM, M_DIM), F32),
                        pltpu.VMEM((M_HEADS, M_DIM, LANES), F32),
                        pltpu.VMEM((1, LANES), F32)],
        compiler_params=_cparams(("arbitrary", "arbitrary")),
        name="mlstm",
    )(um, vm_t, gates, hist8, ct0, n0, m0, tri, cw, cb, wk_bd, wq_bd.T, wk_bd.T, mnorm, mskip)


def _lambda(lq1_ref, lk1_ref, lq2_ref, lk2_ref, lam_init):
    e1 = jnp.exp(jnp.sum(lq1_ref[...] * lk1_ref[...], axis=1, keepdims=True))
    e2 = jnp.exp(jnp.sum(lq2_ref[...] * lk2_ref[...], axis=1, keepdims=True))
    return e1 - e2 + lam_init


def _attn_kernel(*refs, tq, tk, rc, fox, lam_init):
    if fox:
        q_ref, k_ref, v_ref, f_ref, o_ref, qs, s_a, s_b, p_a, p_b, al_a, al_b, m_s, l_s, acc = refs
    else:
        (q_ref, k_ref, v_ref, dn_ref, lq1_ref, lk1_ref, lq2_ref, lk2_ref,
         o_ref, qs, s_a, s_b, p_a, p_b, al_a, al_b, m_s, l_s, acc) = refs
    s_buf, p_buf, al_buf = (s_a, s_b), (p_a, p_b), (al_a, al_b)
    i = pl.program_id(2)
    j_last = ((i + 1) * tq - 1) // tk
    half = LANES // 2
    f_row = 12 + 2 * pl.program_id(1)

    q = q_ref[...].astype(F32)
    lane = lax.broadcasted_iota(jnp.int32, q.shape, 1)
    qs[0:tq, :] = jnp.where(lane < half, q, 0.0).astype(BF16)
    qs[tq:2 * tq, :] = jnp.where(lane >= half, q, 0.0).astype(BF16)
    diag = (lax.broadcasted_iota(jnp.int32, (rc, tk), 1) - lax.broadcasted_iota(jnp.int32, (rc, tk), 0))

    def scores(t, slot):
        k0 = pl.multiple_of(t * tk, tk)
        s_buf[slot][...] = _dot_nt(qs[...], k_ref[pl.ds(k0, tk), :])

    def values(t, slot, first=False):
        k0 = pl.multiple_of(t * tk, tk)
        pv = _dot(p_buf[slot][...], v_ref[pl.ds(k0, tk), :])
        acc[...] = pv if first else al_buf[slot][...] * acc[...] + pv

    def softmax(t, slot, masked, first=False):
        k0 = pl.multiple_of(t * tk, tk)
        if fox:
            f_top = f_ref[pl.ds(f_row, 1), pl.ds(k0, tk)]
            f_bot = f_ref[pl.ds(f_row + 1, 1), pl.ds(k0, tk)]
        for c in range(2 * tq // rc):
            rows = slice(c * rc, (c + 1) * rc)
            s = s_buf[slot][rows, :]
            if fox:
                s = s - (f_top if c * rc < tq else f_bot)
            if masked:
                s = jnp.where(diag <= i * tq + (c * rc) % tq - k0, s, NEG_INF)
            m_cur = jnp.max(s, axis=1, keepdims=True)
            if first:
                m_new = jnp.broadcast_to(m_cur, (rc, LANES))
            else:
                m_prev = m_s[rows, :]
                m_new = jnp.maximum(m_prev, m_cur)
                alpha = jnp.exp(m_prev - m_new)
                al_buf[slot][rows, :] = alpha
            p = jnp.exp(s - jnp.concatenate([m_new] * (tk // LANES), axis=1))
            l_cur = jnp.sum(p, axis=1, keepdims=True)
            l_s[rows, :] = jnp.broadcast_to(l_cur, (rc, LANES)) if first else alpha * l_s[rows, :] + l_cur
            m_s[rows, :] = m_new
            p_buf[slot][rows, :] = p.astype(BF16)

    def step(t, slot):
        scores(t + 1, 1 - slot)
        values(t - 1, 1 - slot)
        softmax(t, slot, False)

    def last(slot):
        values(j_last - 1, 1 - slot)
        softmax(j_last, slot, True)
        values(j_last, slot)

    scores(0, 0)

    @pl.when(j_last == 0)
    def _():
        softmax(0, 0, True, first=True)
        values(0, 0, first=True)

    @pl.when(j_last > 0)
    def _():
        scores(1, 1)
        softmax(0, 0, False, first=True)

    @pl.when(j_last == 1)
    def _():
        values(0, 0, first=True)
        softmax(1, 1, True)
        values(1, 1)

    @pl.when(j_last > 1)
    def _():
        scores(2, 0)
        values(0, 0, first=True)
        softmax(1, 1, False)

    def pair(u, carry):
        step(2 * u + 2, 0)
        step(2 * u + 3, 1)
        return carry

    lax.fori_loop(0, jnp.maximum(j_last - 2, 0) // 2, pair, 0)

    @pl.when((j_last > 1) & (j_last % 2 == 1))
    def _():
        step(j_last - 1, 0)
        last(1)

    @pl.when((j_last > 1) & (j_last % 2 == 0))
    def _():
        last(0)

    o = acc[...] / l_s[...]
    top, bot = o[0:tq, :], o[tq:2 * tq, :]
    if fox:
        o_ref[...] = jnp.where(lane < half, top, bot)
    else:
        lam = _lambda(lq1_ref, lk1_ref, lq2_ref, lk2_ref, lam_init)
        od = top - lam * bot
        od = od * lax.rsqrt(jnp.mean(od * od, axis=1, keepdims=True) + EPS)
        o_ref[...] = od * dn_ref[...] * (1.0 - lam_init)


def _prompt_attn(q, k, v, extra, n_seq, seq, tq, tk, rc, fox, lam_init):
    n, width = q.shape
    groups = width // LANES
    nq = seq // tq
    q_spec = pl.BlockSpec((tq, LANES), lambda b, h, i: (b * nq + i, h))
    kv_spec = pl.BlockSpec((seq, LANES), lambda b, h, i: (b, h))
    if fox:
        extra_specs = [pl.BlockSpec((GATE_ROWS, seq), lambda b, h, i: (0, b))]
    else:
        lam_spec = pl.BlockSpec((1, D_QK), lambda b, h, i: (0, 0))
        extra_specs = [pl.BlockSpec((1, LANES), lambda b, h, i: (0, h))] + [lam_spec] * 4
    return pl.pallas_call(
        functools.partial(_attn_kernel, tq=tq, tk=tk, rc=rc, fox=fox, lam_init=lam_init),
        grid=(n_seq, groups, nq),
        in_specs=[q_spec, kv_spec, kv_spec] + extra_specs,
        out_specs=q_spec,
        out_shape=jax.ShapeDtypeStruct((n, width), F32),
        scratch_shapes=[pltpu.VMEM((2 * tq, LANES), BF16)]
                       + [pltpu.VMEM((2 * tq, tk), F32)] * 2
                       + [pltpu.VMEM((2 * tq, tk), BF16)] * 2
                       + [pltpu.VMEM((2 * tq, LANES), F32)] * 2
                       + [pltpu.VMEM((2 * tq, LANES), F32)] * 3,
        compiler_params=_cparams(("arbitrary",) * 3),
        name="fox_attn" if fox else "diff_attn",
    )(q, k, v, *extra)


D_ROWS = PAGE_SIZE * D_HEADS


def _rows_from_segments(row_vec, n_seg, r):
    out = jnp.zeros((8, LANES), F32)
    for s in range(n_seg):
        seg = jnp.broadcast_to(row_vec[:, s * LANES:(s + 1) * LANES], (8, LANES))
        out = jnp.where(lax.shift_right_logical(r, 1) == s, seg, out)
    return out


def _decode_kernel(pt_ref, *refs, pages_per_step, lam_init):
    G = pages_per_step
    (dq_ref, fq_ref, dkn_ref, dvn_ref, fkn_ref, fvn_ref, gn_ref,
     dn_ref, lq1_ref, lk1_ref, lq2_ref, lk2_ref, tri_ref) = refs[:13]
    page_refs = refs[13:13 + 5 * G]
    od_ref, of_ref = refs[13 + 5 * G:15 + 5 * G]
    qd_s, qf_s, md_s, ld_s, accd, mf_s, lf_s, accf, fcar = refs[15 + 5 * G:]
    j = pl.program_id(1)
    r8 = lax.broadcasted_iota(jnp.int32, (8, LANES), 0)
    l8 = lax.broadcasted_iota(jnp.int32, (8, LANES), 1)
    half_sel = lax.shift_right_logical(l8, 6) == (r8 & 1)

    def head_cols(shape):
        r = lax.broadcasted_iota(jnp.int32, shape, 0)
        c = lax.broadcasted_iota(jnp.int32, shape, 1)
        return lax.shift_right_logical(c, 6) == r

    @pl.when(j == 0)
    def _():
        qd = _rows_from_segments(dq_ref[0].astype(F32), D_HEADS, r8)
        qd_s[...] = jnp.where(half_sel, qd, 0.0).astype(BF16)
        qf = jnp.broadcast_to(fq_ref[0].astype(F32), (8, F_WIDTH))
        qf_s[...] = jnp.where(head_cols(qf.shape), qf, 0.0).astype(BF16)
        md_s[...] = jnp.full_like(md_s, NEG_INF)
        mf_s[...] = jnp.full_like(mf_s, NEG_INF)
        ld_s[...] = jnp.zeros_like(ld_s)
        lf_s[...] = jnp.zeros_like(lf_s)
        accd[...] = jnp.zeros_like(accd)
        accf[...] = jnp.zeros_like(accf)
        fcar[...] = jnp.zeros_like(fcar)

    def online(s, v_refs, width, pv_dot, m_ref, l_ref, acc_ref):
        m_prev = m_ref[...]
        m_new = jnp.maximum(m_prev, jnp.max(s, axis=1, keepdims=True))
        alpha = jnp.exp(m_prev - m_new)
        p = jnp.exp(s - m_new)
        l_ref[...] = alpha * l_ref[...] + jnp.sum(p, axis=1, keepdims=True)
        pb = p.astype(BF16)
        pv = pv_dot(pb[:, 0:width], v_refs[0][0, 0].astype(BF16))
        for g in range(1, G):
            pv = pv + pv_dot(pb[:, g * width:(g + 1) * width], v_refs[g][0, 0].astype(BF16))
        acc_ref[...] = alpha * acc_ref[...] + pv
        m_ref[...] = m_new

    kd_refs = page_refs[0::5]
    vd_refs = page_refs[1::5]
    kf_refs = page_refs[2::5]
    vf_refs = page_refs[3::5]
    fl_refs = page_refs[4::5]

    qd = qd_s[...]
    s_d = jnp.concatenate([_dot_nt(qd, kd_refs[g][0, 0].astype(BF16)) for g in range(G)], axis=1)
    rd = lax.broadcasted_iota(jnp.int32, s_d.shape, 0)
    cd = lax.broadcasted_iota(jnp.int32, s_d.shape, 1)
    s_d = jnp.where((cd & (D_HEADS - 1)) == lax.shift_right_logical(rd, 1), s_d, NEG_INF)
    online(s_d, vd_refs, D_ROWS, _dot, md_s, ld_s, accd)

    qf = qf_s[...]
    tri = tri_ref[...]
    carry = fcar[...]
    parts = []
    for g in range(G):
        fl = fl_refs[g][0, 0]
        hi, mid, lo = _split3(fl)
        cum = _dot(hi, tri) + _dot(mid, tri) + _dot(lo, tri) + carry
        carry = carry + jnp.sum(fl, axis=1, keepdims=True)
        parts.append(_dot(qf, kf_refs[g][0, 0].astype(BF16)) - cum)
    fcar[...] = carry
    online(jnp.concatenate(parts, axis=1), vf_refs, PAGE_SIZE, _dot_nt, mf_s, lf_s, accf)

    @pl.when(j == pl.num_programs(1) - 1)
    def _():
        def add_new(q_s, k_new, v_new, bias, m_ref, l_ref, acc_ref):
            s = jnp.sum(q_s[...].astype(F32) * k_new, axis=1, keepdims=True) - bias
            m_prev = m_ref[...]
            m_new = jnp.maximum(m_prev, s)
            alpha = jnp.exp(m_prev - m_new)
            p = jnp.exp(s - m_new)
            l_new = alpha * l_ref[...] + p
            return (alpha * acc_ref[...] + p * v_new) / l_new

        od_all = add_new(qd_s, _rows_from_segments(dkn_ref[0], D_HEADS, r8),
                         _rows_from_segments(dvn_ref[0], D_HEADS, r8), 0.0, md_s, ld_s, accd)
        f_new = fcar[...] + gn_ref[0]
        of_all = add_new(qf_s, fkn_ref[0], fvn_ref[0], f_new, mf_s, lf_s, accf)

        lam = _lambda(lq1_ref, lk1_ref, lq2_ref, lk2_ref, lam_init)
        parts_d = []
        for h in range(D_HEADS):
            seg = od_all[2 * h:2 * h + 1, :] - lam * od_all[2 * h + 1:2 * h + 2, :]
            parts_d.append(seg * lax.rsqrt(jnp.mean(seg * seg, axis=1, keepdims=True) + EPS))
        od_ref[0] = jnp.concatenate(parts_d, axis=1) * dn_ref[...] * (1.0 - lam_init)
        of_ref[0] = jnp.sum(jnp.where(head_cols(of_all.shape), of_all, 0.0), axis=0, keepdims=True)


def _decode_attn(page_table, layer, dq, fq, dk_new, dv_new, fk_new, fv_new, g_new,
                 d_norm, lq1, lk1, lq2, lk2, tri, c_dk, c_dv, c_fk, c_fv, c_flt,
                 pages_per_step, lam_init):
    nb, n_pages = page_table.shape
    G = pages_per_step
    steps = n_pages // G
    per_b = lambda w: pl.BlockSpec((1, 1, w), lambda b, j, pt: (b, 0, 0))
    const = lambda shape: pl.BlockSpec(shape, lambda b, j, pt: (0,) * len(shape))
    in_specs = [per_b(D_WIDTH), per_b(F_WIDTH), per_b(D_WIDTH), per_b(D_WIDTH),
                per_b(F_WIDTH), per_b(F_WIDTH),
                pl.BlockSpec((1, 8, 1), lambda b, j, pt: (b, 0, 0)),
                const((1, D_WIDTH))] + [const((1, D_QK))] * 4 + [const((PAGE_SIZE, PAGE_SIZE))]
    args = [dq, fq, dk_new, dv_new, fk_new, fv_new, g_new, d_norm, lq1, lk1, lq2, lk2, tri]
    for g in range(G):
        def page(rows, w, g=g):
            return pl.BlockSpec((1, 1, rows, w),
                                lambda b, j, pt: (layer, pt[b, j * G + g], 0, 0))
        in_specs += [page(D_ROWS, LANES), page(D_ROWS, LANES),
                     page(F_WIDTH, PAGE_SIZE), page(F_WIDTH, PAGE_SIZE), page(8, PAGE_SIZE)]
        args += [c_dk, c_dv, c_fk, c_fv, c_flt]
    grid_spec = pltpu.PrefetchScalarGridSpec(
        num_scalar_prefetch=1,
        grid=(nb, steps),
        in_specs=in_specs,
        out_specs=[per_b(D_WIDTH), per_b(F_WIDTH)],
        scratch_shapes=[pltpu.VMEM((8, LANES), BF16), pltpu.VMEM((8, F_WIDTH), BF16),
                        pltpu.VMEM((8, 1), F32), pltpu.VMEM((8, 1), F32),
                        pltpu.VMEM((8, LANES), F32),
                        pltpu.VMEM((8, 1), F32), pltpu.VMEM((8, 1), F32),
                        pltpu.VMEM((8, F_WIDTH), F32),
                        pltpu.VMEM((8, 1), F32)])
    return pl.pallas_call(
        functools.partial(_decode_kernel, pages_per_step=G, lam_init=lam_init),
        grid_spec=grid_spec,
        out_shape=[jax.ShapeDtypeStruct((nb, 1, D_WIDTH), F32),
                   jax.ShapeDtypeStruct((nb, 1, F_WIDTH), F32)],
        compiler_params=_cparams(("arbitrary", "arbitrary")),
        name="decode_attn",
    )(page_table, *args)


def _out_proj_kernel(x_ref, hm_ref, od_ref, of_ref, z_ref, w_ref, g_ref, o_ref):
    h = jnp.concatenate([hm_ref[...], od_ref[...], of_ref[...]], axis=1) * _silu(z_ref[...])
    y = _dot(h.astype(BF16), w_ref[...])
    yn = y * lax.rsqrt(jnp.mean(y * y, axis=-1, keepdims=True) + EPS) * g_ref[...]
    o_ref[...] = x_ref[...] + yn


def _out_proj(x2d, hm, od, of, z, w_out, norm_g, tm):
    n = x2d.shape[0]
    row_spec = lambda w: pl.BlockSpec((tm, w), lambda i: (i, 0))
    const = lambda shape: pl.BlockSpec(shape, lambda i: (0,) * len(shape))
    return pl.pallas_call(
        _out_proj_kernel,
        grid=(n // tm,),
        in_specs=[row_spec(D_MODEL), row_spec(M_WIDTH), row_spec(D_WIDTH), row_spec(F_WIDTH),
                  row_spec(MIX_WIDTH), const((MIX_WIDTH, D_MODEL)), const((1, D_MODEL))],
        out_specs=row_spec(D_MODEL),
        out_shape=jax.ShapeDtypeStruct((n, D_MODEL), F32),
        compiler_params=_cparams(("arbitrary",)),
        name="out_proj",
    )(x2d, hm, od, of, z, w_out, norm_g)


def _block_diag(w):
    hh, d, _ = w.shape
    out = jnp.zeros((hh * d, hh * d), w.dtype)
    for h in range(hh):
        out = out.at[h * d:(h + 1) * d, h * d:(h + 1) * d].set(w[h])
    return out


def _layer_weights(l, w_in, b_i, b_f, b_fox, wq_m, wk_m):
    idx = [0]
    for s in SPLITS:
        idx.append(idx[-1] + s)
    cols = lambda k: w_in[l][:, idx[k]:idx[k + 1]]
    u_m, v_m, ig, fg, dq, dk, dv, fq, fk, fv, ffg, z = [cols(k) for k in range(len(SPLITS))]
    w_main = jnp.concatenate([u_m, v_m, dq * (D_QK ** -0.5), dk, dv, fq * (F_DIM ** -0.5), fk, fv, z],
                             axis=1).astype(BF16)
    w_gate_t = jnp.concatenate([ig, fg, ffg, jnp.zeros((D_MODEL, GATE_ROWS - 12), F32)],
                               axis=1).T.astype(BF16)
    gate_bias = jnp.concatenate([b_i[l], b_f[l], b_fox[l], jnp.zeros((GATE_ROWS - 12,), F32)]
                                ).astype(F32).reshape(GATE_ROWS, 1)
    wq_bd = _block_diag(wq_m[l]).astype(BF16)
    wk_bd = (_block_diag(wk_m[l]) * (M_DIM ** -0.5)).astype(BF16)
    return w_main, v_m.T.astype(BF16), w_gate_t, gate_bias, wq_bd, wk_bd


def kernel(x_prompt, x_sample, cache_dk, cache_dv, cache_fk, cache_fv, cache_flogf, state_mconv, state_mC, state_mn, state_mm, page_table, norm_pre, norm_post, w_in, conv_w, conv_b, wq_m, wk_m, b_i, b_f, m_norm, m_skip, lam_q1, lam_k1, lam_q2, lam_k2, d_norm, b_fox, w_out):
    bp, seq, _ = x_prompt.shape
    nb = x_sample.shape[0]
    n_pool = cache_dk.shape[1]
    n_p = bp * seq
    tm_p = 512
    tq, tk = 256, 512
    rc = 64
    pages_per_step = 16
    L = M_CHUNK

    xp = x_prompt.reshape(n_p, D_MODEL)
    ns = LANES
    xs = jnp.pad(x_sample.reshape(nb, D_MODEL), ((0, ns - nb), (0, 0)))
    c_dk = cache_dk.reshape(DEPTH, n_pool, D_ROWS, LANES)
    c_dv = cache_dv.reshape(DEPTH, n_pool, D_ROWS, LANES)
    c_fk = jnp.transpose(cache_fk, (0, 1, 3, 4, 2)).reshape(DEPTH, n_pool, F_WIDTH, PAGE_SIZE)
    c_fv = jnp.transpose(cache_fv, (0, 1, 3, 4, 2)).reshape(DEPTH, n_pool, F_WIDTH, PAGE_SIZE)
    c_flt = jnp.pad(jnp.swapaxes(cache_flogf.astype(F32), 2, 3), ((0, 0), (0, 0), (0, 8 - F_HEADS), (0, 0)))

    tri_p = (lax.broadcasted_iota(jnp.int32, (tm_p, tm_p), 0)
             <= lax.broadcasted_iota(jnp.int32, (tm_p, tm_p), 1)).astype(BF16)
    tri_s = jnp.eye(ns, dtype=BF16)
    tri_page = tri_p[:PAGE_SIZE, :PAGE_SIZE]

    zero_hist = jnp.zeros((bp, 8, M_WIDTH), F32)
    zero_c = jnp.zeros((bp, M_HEADS, M_DIM, M_DIM), F32)
    zero_n = jnp.zeros((bp, M_HEADS, M_DIM, LANES), F32)
    zero_m = jnp.zeros((bp, 1, LANES), F32)

    new_p, new_s = [], []
    for l in range(DEPTH):
        lam_init = 0.8 - 0.6 * math.exp(-0.3 * l)
        w_main, w_vm_t, w_gate_t, gate_bias, wq_bd, wk_bd = _layer_weights(l, w_in, b_i, b_f, b_fox, wq_m, wk_m)
        w_out_l = w_out[l].astype(BF16)
        g_pre = norm_pre[l].reshape(1, D_MODEL)
        g_post = norm_post[l].reshape(1, D_MODEL)
        cw, cb = conv_w[l], conv_b[l].reshape(1, M_WIDTH)
        mnorm, mskip = m_norm[l].reshape(1, M_WIDTH), m_skip[l].reshape(1, M_WIDTH)
        dn = d_norm[l].reshape(1, D_WIDTH)
        lams = [a[l].reshape(1, D_QK).astype(F32) for a in (lam_q1, lam_k1, lam_q2, lam_k2)]

        (um, vm, dq, dk32, dk16, dv32, dv16, fq, fk32, fk16, fv32, fv16, z, gates) = _in_proj(
            xp, g_pre, w_main, w_vm_t, w_gate_t, gate_bias, tri_p, tm_p, seq // tm_p)
        gates_seq = gates.reshape(GATE_ROWS, bp, seq).transpose(1, 0, 2)
        hm, ct_fin, n_fin, m_fin = _mlstm(um.reshape(bp, seq, M_WIDTH), vm, gates_seq,
                                          zero_hist, zero_c, zero_n, zero_m, tri_page,
                                          cw, cb, wq_bd, wk_bd, mnorm, mskip)
        c_fin, n_fin = jnp.swapaxes(ct_fin, 2, 3), n_fin[..., 0]
        hm = hm.reshape(n_p, M_WIDTH)
        od = _prompt_attn(dq, dk16, dv16, [dn] + lams, bp, seq, tq, tk, rc, False, lam_init)
        of = _prompt_attn(fq, fk16, fv16, [gates], bp, seq, tq, tk, rc, True, lam_init)
        xp = _out_proj(xp, hm, od, of, z, w_out_l, g_post, tm_p)
        new_p.append((dk32.reshape(bp, seq, D_HEADS, 2 * D_QK), dv32.reshape(bp, seq, D_HEADS, D_V),
                      fk32.reshape(bp, seq, F_HEADS, F_DIM), fv32.reshape(bp, seq, F_HEADS, F_DIM),
                      gates[8:12].T.reshape(bp, seq, F_HEADS),
                      um.reshape(bp, seq, M_WIDTH)[:, seq - (CONV_W - 1):],
                      c_fin, n_fin, m_fin[:, 0, :M_HEADS]))

        outs_s = _in_proj(xs, g_pre, w_main, w_vm_t, w_gate_t, gate_bias, tri_s, ns, 1)
        (um, vm, dq, dk32, _, dv32, _, fq, fk32, _, fv32, _, z, gates) = [
            a[:, :nb] if k in (1, 13) else a[:nb * (a.shape[0] // ns)] for k, a in enumerate(outs_s)]
        z = outs_s[12]
        dk32, dv32 = dk32.reshape(nb, D_WIDTH), dv32.reshape(nb, D_WIDTH)
        pad_tok = lambda a: jnp.pad(a[:, None, :], ((0, 0), (0, L - 1), (0, 0)))
        noop = jnp.where(jnp.arange(GATE_ROWS) < M_HEADS, NEG_INF, 0.0).astype(F32)
        gates_pad = jnp.concatenate(
            [gates.T[:, :, None], jnp.broadcast_to(noop[None, :, None], (nb, GATE_ROWS, L - 1))], axis=2)
        hist = state_mconv[l].astype(F32)
        hist8 = jnp.pad(hist, ((0, 0), (8 - (CONV_W - 1), 0), (0, 0)))
        m0 = jnp.pad(state_mm[l].astype(F32), ((0, 0), (0, LANES - M_HEADS)))[:, None, :]
        vm_pad = jnp.pad(vm[:, :, None], ((0, 0), (0, 0), (0, L - 1))).reshape(M_WIDTH, nb * L)
        n0 = jnp.broadcast_to(state_mn[l].astype(F32)[..., None], (nb, M_HEADS, M_DIM, LANES))
        hm, ct_fin, n_fin, m_fin = _mlstm(pad_tok(um), vm_pad, gates_pad, hist8,
                                          jnp.swapaxes(state_mC[l].astype(F32), 2, 3), n0, m0, tri_page,
                                          cw, cb, wq_bd, wk_bd, mnorm, mskip)
        c_fin, n_fin = jnp.swapaxes(ct_fin, 2, 3), n_fin[..., 0]
        hm = hm[:, 0]
        flog_new = gates[8:12].T
        g_new = jnp.pad(flog_new, ((0, 0), (0, 8 - F_HEADS)))[:, :, None]
        od, of = _decode_attn(page_table, l, dq[:, None, :], fq[:, None, :], dk32[:, None, :],
                              dv32[:, None, :], fk32[:, None, :], fv32[:, None, :], g_new,
                              dn, *lams, tri_page, c_dk, c_dv, c_fk, c_fv, c_flt, pages_per_step, lam_init)
        pad_rows = lambda a: jnp.pad(a, ((0, ns - nb), (0, 0)))
        xs = _out_proj(xs, pad_rows(hm), pad_rows(od[:, 0]), pad_rows(of[:, 0]), z, w_out_l, g_post, ns)
        new_s.append((dk32.reshape(nb, 1, D_HEADS, 2 * D_QK), dv32.reshape(nb, 1, D_HEADS, D_V),
                      fk32.reshape(nb, 1, F_HEADS, F_DIM), fv32.reshape(nb, 1, F_HEADS, F_DIM),
                      flog_new.reshape(nb, 1, F_HEADS),
                      jnp.concatenate([hist[:, 1:], um[:, None, :]], axis=1),
                      c_fin, n_fin, m_fin[:, 0, :M_HEADS]))

    stack = lambda states, k: jnp.stack([st[k] for st in states])
    return ((xp.reshape(bp, seq, D_MODEL), xs[:nb].reshape(nb, 1, D_MODEL))
            + tuple(stack(new_p, k) for k in range(9))
            + tuple(stack(new_s, k) for k in range(9)))
```

```python
import functools
import math

import jax
import jax.numpy as jnp
from jax import lax
from jax.experimental import pallas as pl
from jax.experimental.pallas import tpu as pltpu

D_MODEL = 1024
DEPTH = 4
PAGE_SIZE = 128
M_HEADS = 4
M_DIM = 64
M_WIDTH = M_HEADS * M_DIM
CONV_W = 4
M_CHUNK = 128
D_HEADS = 4
D_QK = 64
D_V = 2 * D_QK
D_WIDTH = D_HEADS * D_V
F_HEADS = 4
F_DIM = 64
F_WIDTH = F_HEADS * F_DIM
MIX_WIDTH = M_WIDTH + D_WIDTH + F_WIDTH
EPS = 1e-6
SPLITS = (M_WIDTH, M_WIDTH, M_HEADS, M_HEADS, D_WIDTH, D_WIDTH, D_WIDTH,
          F_WIDTH, F_WIDTH, F_WIDTH, F_HEADS, MIX_WIDTH)

LANES = 128
GATE_ROWS = 16
VMEM_LIMIT = 56 * 1024 * 1024

F32 = jnp.float32
BF16 = jnp.bfloat16
NEG_INF = float("-inf")

_MAIN = (("um", M_WIDTH), ("vm", M_WIDTH), ("dq", D_WIDTH), ("dk", D_WIDTH), ("dv", D_WIDTH),
         ("fq", F_WIDTH), ("fk", F_WIDTH), ("fv", F_WIDTH), ("z", MIX_WIDTH))
_MAIN_OFF = {}
_o = 0
for _n, _w in _MAIN:
    _MAIN_OFF[_n] = (_o, _w)
    _o += _w
MAIN_WIDTH = _o


def _cparams(sem):
    return pltpu.CompilerParams(dimension_semantics=sem, vmem_limit_bytes=VMEM_LIMIT)


def _log_sigmoid(x):
    return jnp.minimum(x, 0.0) - jnp.log1p(jnp.exp(-jnp.abs(x)))


def _silu(x):
    return x * (1.0 / (1.0 + jnp.exp(-x)))


def _dot(a, b):
    return jnp.dot(a, b, preferred_element_type=F32)


def _dot_nt(a, b):
    return lax.dot_general(a, b, (((1,), (1,)), ((), ())), preferred_element_type=F32)


def _split3(x):
    hi = x.astype(BF16)
    r1 = x - hi.astype(F32)
    mid = r1.astype(BF16)
    lo = (r1 - mid.astype(F32)).astype(BF16)
    return hi, mid, lo


def _in_proj_kernel(x_ref, g_ref, w_ref, wvt_ref, wg_ref, gb_ref, tri_ref,
                    um_ref, vmt_ref, dq_ref, dk32_ref, dk16_ref, dv32_ref, dv16_ref,
                    fq_ref, fk32_ref, fk16_ref, fv32_ref, fv16_ref, z_ref, gates_ref, fkt_ref, fvt_ref,
                    carry_ref, *, tiles_per_seq):
    i = pl.program_id(0)
    x = x_ref[...]
    xn = x * lax.rsqrt(jnp.mean(x * x, axis=-1, keepdims=True) + EPS) * g_ref[...]
    xb = xn.astype(BF16)

    def proj(name):
        off, width = _MAIN_OFF[name]
        return _dot(xb, w_ref[:, off:off + width])

    um_ref[...] = proj("um")
    vmt_ref[...] = _dot_nt(wvt_ref[...], xb)
    dq_ref[...] = proj("dq").astype(BF16)
    dk = proj("dk")
    tm = x.shape[0]
    for h in range(D_HEADS):
        dk32_ref[pl.ds(h, tm, stride=D_HEADS), :] = dk[:, h * D_V:(h + 1) * D_V]
    dk16_ref[...] = dk.astype(BF16)
    dv = proj("dv")
    for h in range(D_HEADS):
        dv32_ref[pl.ds(h, tm, stride=D_HEADS), :] = dv[:, h * D_V:(h + 1) * D_V]
    dv16_ref[...] = dv.astype(BF16)
    fq_ref[...] = proj("fq").astype(BF16)
    fk = proj("fk")
    fk32_ref[...] = fk
    fk16_ref[...] = fk.astype(BF16)
    fv = proj("fv")
    fv32_ref[...] = fv
    fv16_ref[...] = fv.astype(BF16)
    z_ref[...] = proj("z")
    for name, t_ref in (("fk", fkt_ref), ("fv", fvt_ref)):
        off, width = _MAIN_OFF[name]
        t_ref[0] = lax.dot_general(w_ref[:, off:off + width], xb, (((0,), (1,)), ((), ())),
                                   preferred_element_type=F32)

    pre = _dot_nt(wg_ref[...], xb) + gb_ref[...]
    row = lax.broadcasted_iota(jnp.int32, pre.shape, 0)
    act = jnp.where(row < M_HEADS, pre, _log_sigmoid(pre))

    @pl.when(i % tiles_per_seq == 0)
    def _():
        carry_ref[...] = jnp.zeros_like(carry_ref)

    hi, mid, lo = _split3(act)
    tri = tri_ref[...]
    cs = _dot(hi, tri) + _dot(mid, tri) + _dot(lo, tri) + carry_ref[:, 0:1]
    carry_ref[...] = jnp.broadcast_to(cs[:, cs.shape[1] - 1:], carry_ref.shape)
    shifted = pltpu.roll(cs, 4, axis=0)
    gates_ref[...] = jnp.where((row >= 12), shifted, jnp.where(row < 12, act, 0.0))


def _in_proj(x2d, norm_g, w_main, w_vm_t, w_gate_t, gate_bias, tri, tm, tiles_per_seq):
    n = x2d.shape[0]
    grid = (n // tm,)
    row_spec = lambda w: pl.BlockSpec((tm, w), lambda i: (i, 0))
    const = lambda shape: pl.BlockSpec(shape, lambda i: (0,) * len(shape))
    out_shape = [
        jax.ShapeDtypeStruct((n, M_WIDTH), F32),
        jax.ShapeDtypeStruct((M_WIDTH, n), F32),
        jax.ShapeDtypeStruct((n, D_WIDTH), BF16),
        jax.ShapeDtypeStruct((n * D_HEADS, D_V), F32),
        jax.ShapeDtypeStruct((n, D_WIDTH), BF16),
        jax.ShapeDtypeStruct((n * D_HEADS, D_V), F32),
        jax.ShapeDtypeStruct((n, D_WIDTH), BF16),
        jax.ShapeDtypeStruct((n, F_WIDTH), BF16),
        jax.ShapeDtypeStruct((n, F_WIDTH), F32),
        jax.ShapeDtypeStruct((n, F_WIDTH), BF16),
        jax.ShapeDtypeStruct((n, F_WIDTH), F32),
        jax.ShapeDtypeStruct((n, F_WIDTH), BF16),
        jax.ShapeDtypeStruct((n, MIX_WIDTH), F32),
        jax.ShapeDtypeStruct((GATE_ROWS, n), F32),
    ]
    out_specs = [pl.BlockSpec((tm * (s.shape[0] // n), s.shape[1]), lambda i: (i, 0)) for s in out_shape[:-1]]
    out_specs[1] = pl.BlockSpec((M_WIDTH, tm), lambda i: (0, i))
    out_specs.append(pl.BlockSpec((GATE_ROWS, tm), lambda i: (0, i)))
    n_seq = n // (tm * tiles_per_seq)
    for _ in range(2):
        out_shape.append(jax.ShapeDtypeStruct((n_seq, F_WIDTH, tm * tiles_per_seq), F32))
        out_specs.append(pl.BlockSpec((1, F_WIDTH, tm), lambda i: (i // tiles_per_seq, 0, i % tiles_per_seq)))
    return pl.pallas_call(
        functools.partial(_in_proj_kernel, tiles_per_seq=tiles_per_seq),
        grid=grid,
        in_specs=[row_spec(D_MODEL), const((1, D_MODEL)), const((D_MODEL, MAIN_WIDTH)),
                  const((M_WIDTH, D_MODEL)), const((GATE_ROWS, D_MODEL)), const((GATE_ROWS, 1)),
                  const((tm, tm))],
        out_specs=out_specs,
        out_shape=out_shape,
        scratch_shapes=[pltpu.VMEM((GATE_ROWS, LANES), F32)],
        compiler_params=_cparams(("arbitrary",)),
        name="in_proj",
    )(x2d, norm_g, w_main, w_vm_t, w_gate_t, gate_bias, tri)


def _mlstm_kernel(um_ref, vt_ref, gates_ref, hist_ref, ct0_ref, n0_ref, m0_ref,
                  tri_ref, cw_ref, cb_ref, wk_ref, wqt_ref, wkt_ref, mnorm_ref, mskip_ref,
                  hm_ref, ct_out_ref, n_out_ref, m_out_ref,
                  ubuf, ct_s, n_s, m_s):
    c = pl.program_id(1)
    L = M_CHUNK

    @pl.when(c == 0)
    def _():
        ubuf[0:8, :] = hist_ref[0]
        ct_s[...] = ct0_ref[0]
        n_s[...] = n0_ref[0]
        m_s[...] = m0_ref[0]

    ubuf[8:8 + L, :] = um_ref[0]
    conv = cb_ref[...]
    for j in range(CONV_W):
        conv = conv + ubuf[5 + j:5 + j + L, :] * cw_ref[j:j + 1, :]
    ubuf[0:8, :] = ubuf[L:L + 8, :]
    xc = _silu(conv)
    xb = xc.astype(BF16)
    k_all = _dot(xb, wk_ref[...]).astype(BF16)
    qt_all = _dot_nt(wqt_ref[...], xb)
    kt_all = _dot_nt(wkt_ref[...], xb)
    vt_all = vt_ref[...]

    gates = gates_ref[0][0:8, :]
    row8 = lax.broadcasted_iota(jnp.int32, gates.shape, 0)
    hi, mid, lo = _split3(jnp.where(row8 >= M_HEADS, gates, 0.0))
    tri = tri_ref[...]
    bcum = _dot(hi, tri) + _dot(mid, tri) + _dot(lo, tri)
    a_rows = gates - pltpu.roll(bcum, M_HEADS, axis=0)

    si = lax.broadcasted_iota(jnp.int32, (L, L), 0)
    ti = lax.broadcasted_iota(jnp.int32, (L, L), 1)
    causal = si <= ti
    eye = si == ti
    zeros_half = jnp.zeros((M_DIM, L), BF16)

    m_prev_all = m_s[...]
    m_new_all = m_prev_all
    lane1 = lax.broadcasted_iota(jnp.int32, m_prev_all.shape, 1)
    ct_prev_all = [ct_s[h] for h in range(M_HEADS)]
    n_prev_all = [n_s[h] for h in range(M_HEADS)]
    ct_new, n_new, hts = [], [], []
    for h in range(M_HEADS):
        rows = slice(h * M_DIM, (h + 1) * M_DIM)
        pair = slice((h // 2) * LANES, (h // 2 + 1) * LANES)
        qt_h = qt_all[rows, :]
        qt_b = qt_h.astype(BF16)
        kt_b = kt_all[rows, :].astype(BF16)
        vt_b = vt_all[rows, :].astype(BF16)
        m_prev = m_prev_all[:, h:h + 1]
        ct_prev = ct_prev_all[h]
        n_prev = n_prev_all[h]

        a_row = a_rows[h:h + 1, :]
        b_row = bcum[M_HEADS + h:M_HEADS + h + 1, :]
        a_col = jnp.sum(jnp.where(eye, a_row, 0.0), axis=1, keepdims=True)
        g_row = jnp.maximum(jnp.max(jnp.where(causal, a_col, NEG_INF), axis=0, keepdims=True), m_prev)
        m_row = b_row + g_row
        w_inter = jnp.exp(m_prev - g_row)
        wt = jnp.exp(jnp.where(causal, a_col - g_row, NEG_INF))
        qt_pad = jnp.concatenate([qt_b, zeros_half] if h % 2 == 0 else [zeros_half, qt_b], axis=0)
        st = _dot(k_all[:, pair], qt_pad) * wt
        num_t = _dot(vt_b, st.astype(BF16)) + _dot(ct_prev.astype(BF16), qt_b) * w_inter
        qn = jnp.sum(qt_h * n_prev, axis=0, keepdims=True)
        den = w_inter * qn + jnp.sum(st, axis=0, keepdims=True)
        ht = num_t / jnp.maximum(jnp.abs(den), jnp.exp(-m_row))
        hts.append(ht * lax.rsqrt(jnp.mean(ht * ht, axis=0, keepdims=True) + EPS))

        g_last = g_row[:, L - 1:L]
        decay = jnp.exp(m_prev - g_last)
        w_last = jnp.exp(a_row - g_last)
        vtw = (vt_all[rows, :] * w_last).astype(BF16)
        ct_new.append(decay * ct_prev + _dot_nt(vtw, kt_b))
        n_new.append(decay * n_prev + jnp.sum(kt_all[rows, :] * w_last, axis=1, keepdims=True))
        m_new_all = jnp.where(lane1 == h, m_row[:, L - 1:L], m_new_all)
    for h in range(M_HEADS):
        ct_s[h] = ct_new[h]
        n_s[h] = n_new[h]
    m_s[...] = m_new_all
    hn = jnp.concatenate(hts, axis=0).T
    hm_ref[0] = hn * mnorm_ref[...] + mskip_ref[...] * xc

    @pl.when(c == pl.num_programs(1) - 1)
    def _():
        ct_out_ref[0] = ct_s[...]
        n_out_ref[0] = n_s[...]
        m_out_ref[0] = m_s[...]


def _mlstm(um, vm_t, gates, hist8, ct0, n0, m0, tri, cw, cb, wq_bd, wk_bd, mnorm, mskip):
    n_seq, seq, _ = um.shape
    L = M_CHUNK
    n_chunks = seq // L
    tok = pl.BlockSpec((1, L, M_WIDTH), lambda b, c: (b, c, 0))
    const = lambda shape: pl.BlockSpec(shape, lambda b, c: (0,) * len(shape))
    per_seq = lambda shape: pl.BlockSpec((1,) + shape, lambda b, c: (b,) + (0,) * len(shape))
    return pl.pallas_call(
        _mlstm_kernel,
        grid=(n_seq, n_chunks),
        in_specs=[tok,
                  pl.BlockSpec((M_WIDTH, L), lambda b, c: (0, b * n_chunks + c)),
                  pl.BlockSpec((1, GATE_ROWS, L), lambda b, c: (b, 0, c)),
                  per_seq((8, M_WIDTH)), per_seq((M_HEADS, M_DIM, M_DIM)),
                  per_seq((M_HEADS, M_DIM, LANES)), per_seq((1, LANES)),
                  const((L, L)), const((CONV_W, M_WIDTH)), const((1, M_WIDTH)),
                  const((M_WIDTH, M_WIDTH)), const((M_WIDTH, M_WIDTH)), const((M_WIDTH, M_WIDTH)),
                  const((1, M_WIDTH)), const((1, M_WIDTH))],
        out_specs=[tok, per_seq((M_HEADS, M_DIM, M_DIM)), per_seq((M_HEADS, M_DIM, LANES)),
                   per_seq((1, LANES))],
        out_shape=[jax.ShapeDtypeStruct((n_seq, seq, M_WIDTH), F32),
                   jax.ShapeDtypeStruct((n_seq, M_HEADS, M_DIM, M_DIM), F32),
                   jax.ShapeDtypeStruct((n_seq, M_HEADS, M_DIM, LANES), F32),
                   jax.ShapeDtypeStruct((n_seq, 1, LANES), F32)],
        scratch_shapes=[pltpu.VMEM((L + 8, M_WIDTH), F32),
                        pltpu.VMEM((M_HEADS, M_DIM, M_DIM), F32),
                        pltpu.VMEM((M_HEADS, M_DIM, LANES), F32),
                        pltpu.VMEM((1, LANES), F32)],
        compiler_params=_cparams(("arbitrary", "arbitrary")),
        name="mlstm",
    )(um, vm_t, gates, hist8, ct0, n0, m0, tri, cw, cb, wk_bd, wq_bd.T, wk_bd.T, mnorm, mskip)


def _lambda(lq1_ref, lk1_ref, lq2_ref, lk2_ref, lam_init):
    e1 = jnp.exp(jnp.sum(lq1_ref[...] * lk1_ref[...], axis=1, keepdims=True))
    e2 = jnp.exp(jnp.sum(lq2_ref[...] * lk2_ref[...], axis=1, keepdims=True))
    return e1 - e2 + lam_init


def _attn_kernel(*refs, tq, tk, rc, fox, lam_init):
    if fox:
        q_ref, k_ref, v_ref, f_ref, o_ref, qs, s_a, s_b, p_a, p_b, al_a, al_b, m_s, l_s, acc = refs
    else:
        (q_ref, k_ref, v_ref, dn_ref, lq1_ref, lk1_ref, lq2_ref, lk2_ref,
         o_ref, qs, s_a, s_b, p_a, p_b, al_a, al_b, m_s, l_s, acc) = refs
    s_buf, p_buf, al_buf = (s_a, s_b), (p_a, p_b), (al_a, al_b)
    i = pl.program_id(2)
    j_last = ((i + 1) * tq - 1) // tk
    half = LANES // 2
    f_row = 12 + 2 * pl.program_id(1)

    q = q_ref[...].astype(F32)
    lane = lax.broadcasted_iota(jnp.int32, q.shape, 1)
    qs[0:tq, :] = jnp.where(lane < half, q, 0.0).astype(BF16)
    qs[tq:2 * tq, :] = jnp.where(lane >= half, q, 0.0).astype(BF16)
    diag = (lax.broadcasted_iota(jnp.int32, (rc, tk), 1) - lax.broadcasted_iota(jnp.int32, (rc, tk), 0))

    def scores(t, slot):
        k0 = pl.multiple_of(t * tk, tk)
        s_buf[slot][...] = _dot_nt(qs[...], k_ref[pl.ds(k0, tk), :])

    def values(t, slot, first=False):
        k0 = pl.multiple_of(t * tk, tk)
        pv = _dot(p_buf[slot][...], v_ref[pl.ds(k0, tk), :])
        acc[...] = pv if first else al_buf[slot][...] * acc[...] + pv

    def softmax(t, slot, masked, first=False):
        k0 = pl.multiple_of(t * tk, tk)
        if fox:
            f_top = f_ref[pl.ds(f_row, 1), pl.ds(k0, tk)]
            f_bot = f_ref[pl.ds(f_row + 1, 1), pl.ds(k0, tk)]
        for c in range(2 * tq // rc):
            rows = slice(c * rc, (c + 1) * rc)
            s = s_buf[slot][rows, :]
            if fox:
                s = s - (f_top if c * rc < tq else f_bot)
            if masked:
                s = jnp.where(diag <= i * tq + (c * rc) % tq - k0, s, NEG_INF)
            m_cur = jnp.max(s, axis=1, keepdims=True)
            if first:
                m_new = jnp.broadcast_to(m_cur, (rc, LANES))
            else:
                m_prev = m_s[rows, :]
                m_new = jnp.maximum(m_prev, m_cur)
                alpha = jnp.exp(m_prev - m_new)
                al_buf[slot][rows, :] = alpha
            p = jnp.exp(s - jnp.concatenate([m_new] * (tk // LANES), axis=1))
            l_cur = jnp.sum(p, axis=1, keepdims=True)
            l_s[rows, :] = jnp.broadcast_to(l_cur, (rc, LANES)) if first else alpha * l_s[rows, :] + l_cur
            m_s[rows, :] = m_new
            p_buf[slot][rows, :] = p.astype(BF16)

    def step(t, slot):
        scores(t + 1, 1 - slot)
        values(t - 1, 1 - slot)
        softmax(t, slot, False)

    def last(slot):
        values(j_last - 1, 1 - slot)
        softmax(j_last, slot, True)
        values(j_last, slot)

    scores(0, 0)

    @pl.when(j_last == 0)
    def _():
        softmax(0, 0, True, first=True)
        values(0, 0, first=True)

    @pl.when(j_last > 0)
    def _():
        scores(1, 1)
        softmax(0, 0, False, first=True)

    @pl.when(j_last == 1)
    def _():
        values(0, 0, first=True)
        softmax(1, 1, True)
        values(1, 1)

    @pl.when(j_last > 1)
    def _():
        scores(2, 0)
        values(0, 0, first=True)
        softmax(1, 1, False)

    def pair(u, carry):
        step(2 * u + 2, 0)
        step(2 * u + 3, 1)
        return carry

    lax.fori_loop(0, jnp.maximum(j_last - 2, 0) // 2, pair, 0)

    @pl.when((j_last > 1) & (j_last % 2 == 1))
    def _():
        step(j_last - 1, 0)
        last(1)

    @pl.when((j_last > 1) & (j_last % 2 == 0))
    def _():
        last(0)

    o = acc[...] / l_s[...]
    top, bot = o[0:tq, :], o[tq:2 * tq, :]
    if fox:
        o_ref[...] = jnp.where(lane < half, top, bot)
    else:
        lam = _lambda(lq1_ref, lk1_ref, lq2_ref, lk2_ref, lam_init)
        od = top - lam * bot
        od = od * lax.rsqrt(jnp.mean(od * od, axis=1, keepdims=True) + EPS)
        o_ref[...] = od * dn_ref[...] * (1.0 - lam_init)


def _prompt_attn(q, k, v, extra, n_seq, seq, tq, tk, rc, fox, lam_init):
    n, width = q.shape
    groups = width // LANES
    nq = seq // tq
    q_spec = pl.BlockSpec((tq, LANES), lambda b, h, i: (b * nq + i, h))
    kv_spec = pl.BlockSpec((seq, LANES), lambda b, h, i: (b, h))
    if fox:
        extra_specs = [pl.BlockSpec((GATE_ROWS, seq), lambda b, h, i: (0, b))]
    else:
        lam_spec = pl.BlockSpec((1, D_QK), lambda b, h, i: (0, 0))
        extra_specs = [pl.BlockSpec((1, LANES), lambda b, h, i: (0, h))] + [lam_spec] * 4
    return pl.pallas_call(
        functools.partial(_attn_kernel, tq=tq, tk=tk, rc=rc, fox=fox, lam_init=lam_init),
        grid=(n_seq, groups, nq),
        in_specs=[q_spec, kv_spec, kv_spec] + extra_specs,
        out_specs=q_spec,
        out_shape=jax.ShapeDtypeStruct((n, width), F32),
        scratch_shapes=[pltpu.VMEM((2 * tq, LANES), BF16)]
                       + [pltpu.VMEM((2 * tq, tk), F32)] * 2
                       + [pltpu.VMEM((2 * tq, tk), BF16)] * 2
                       + [pltpu.VMEM((2 * tq, LANES), F32)] * 2
                       + [pltpu.VMEM((2 * tq, LANES), F32)] * 3,
        compiler_params=_cparams(("arbitrary",) * 3),
        name="fox_attn" if fox else "diff_attn",
    )(q, k, v, *extra)


D_ROWS = PAGE_SIZE * D_HEADS


def _rows_from_segments(row_vec, n_seg, r):
    out = jnp.zeros((8, LANES), F32)
    for s in range(n_seg):
        seg = jnp.broadcast_to(row_vec[:, s * LANES:(s + 1) * LANES], (8, LANES))
        out = jnp.where(lax.shift_right_logical(r, 1) == s, seg, out)
    return out


def _decode_kernel(pt_ref, *refs, pages_per_step, lam_init):
    G = pages_per_step
    (dq_ref, fq_ref, dkn_ref, dvn_ref, fkn_ref, fvn_ref, gn_ref,
     dn_ref, lq1_ref, lk1_ref, lq2_ref, lk2_ref, tri_ref) = refs[:13]
    page_refs = refs[13:13 + 5 * G]
    od_ref, of_ref = refs[13 + 5 * G:15 + 5 * G]
    qd_s, qf_s, md_s, ld_s, accd, mf_s, lf_s, accf, fcar = refs[15 + 5 * G:]
    j = pl.program_id(1)
    r8 = lax.broadcasted_iota(jnp.int32, (8, LANES), 0)
    l8 = lax.broadcasted_iota(jnp.int32, (8, LANES), 1)
    half_sel = lax.shift_right_logical(l8, 6) == (r8 & 1)

    def head_cols(shape):
        r = lax.broadcasted_iota(jnp.int32, shape, 0)
        c = lax.broadcasted_iota(jnp.int32, shape, 1)
        return lax.shift_right_logical(c, 6) == r

    @pl.when(j == 0)
    def _():
        qd = _rows_from_segments(dq_ref[0].astype(F32), D_HEADS, r8)
        qd_s[...] = jnp.where(half_sel, qd, 0.0).astype(BF16)
        qf = jnp.broadcast_to(fq_ref[0].astype(F32), (8, F_WIDTH))
        qf_s[...] = jnp.where(head_cols(qf.shape), qf, 0.0).astype(BF16)
        md_s[...] = jnp.full_like(md_s, NEG_INF)
        mf_s[...] = jnp.full_like(mf_s, NEG_INF)
        ld_s[...] = jnp.zeros_like(ld_s)
        lf_s[...] = jnp.zeros_like(lf_s)
        accd[...] = jnp.zeros_like(accd)
        accf[...] = jnp.zeros_like(accf)
        fcar[...] = jnp.zeros_like(fcar)

    def online(s, v_refs, width, pv_dot, m_ref, l_ref, acc_ref):
        m_prev = m_ref[...]
        m_new = jnp.maximum(m_prev, jnp.max(s, axis=1, keepdims=True))
        alpha = jnp.exp(m_prev - m_new)
        p = jnp.exp(s - m_new)
        l_ref[...] = alpha * l_ref[...] + jnp.sum(p, axis=1, keepdims=True)
        pb = p.astype(BF16)
        pv = pv_dot(pb[:, 0:width], v_refs[0][0, 0].astype(BF16))
        for g in range(1, G):
            pv = pv + pv_dot(pb[:, g * width:(g + 1) * width], v_refs[g][0, 0].astype(BF16))
        acc_ref[...] = alpha * acc_ref[...] + pv
        m_ref[...] = m_new

    kd_refs = page_refs[0::5]
    vd_refs = page_refs[1::5]
    kf_refs = page_refs[2::5]
    vf_refs = page_refs[3::5]
    fl_refs = page_refs[4::5]

    qd = qd_s[...]
    s_d = jnp.concatenate([_dot_nt(qd, kd_refs[g][0, 0].astype(BF16)) for g in range(G)], axis=1)
    rd = lax.broadcasted_iota(jnp.int32, s_d.shape, 0)
    cd = lax.broadcasted_iota(jnp.int32, s_d.shape, 1)
    s_d = jnp.where((cd & (D_HEADS - 1)) == lax.shift_right_logical(rd, 1), s_d, NEG_INF)
    online(s_d, vd_refs, D_ROWS, _dot, md_s, ld_s, accd)

    qf = qf_s[...]
    tri = tri_ref[...]
    carry = fcar[...]
    parts = []
    for g in range(G):
        fl = fl_refs[g][0, 0]
        hi, mid, lo = _split3(fl)
        cum = _dot(hi, tri) + _dot(mid, tri) + _dot(lo, tri) + carry
        carry = carry + jnp.sum(fl, axis=1, keepdims=True)
        parts.append(_dot(qf, kf_refs[g][0, 0].astype(BF16)) - cum)
    fcar[...] = carry
    online(jnp.concatenate(parts, axis=1), vf_refs, PAGE_SIZE, _dot_nt, mf_s, lf_s, accf)

    @pl.when(j == pl.num_programs(1) - 1)
    def _():
        def add_new(q_s, k_new, v_new, bias, m_ref, l_ref, acc_ref):
            s = jnp.sum(q_s[...].astype(F32) * k_new, axis=1, keepdims=True) - bias
            m_prev = m_ref[...]
            m_new = jnp.maximum(m_prev, s)
            alpha = jnp.exp(m_prev - m_new)
            p = jnp.exp(s - m_new)
            l_new = alpha * l_ref[...] + p
            return (alpha * acc_ref[...] + p * v_new) / l_new

        od_all = add_new(qd_s, _rows_from_segments(dkn_ref[0], D_HEADS, r8),
                         _rows_from_segments(dvn_ref[0], D_HEADS, r8), 0.0, md_s, ld_s, accd)
        f_new = fcar[...] + gn_ref[0]
        of_all = add_new(qf_s, fkn_ref[0], fvn_ref[0], f_new, mf_s, lf_s, accf)

        lam = _lambda(lq1_ref, lk1_ref, lq2_ref, lk2_ref, lam_init)
        parts_d = []
        for h in range(D_HEADS):
            seg = od_all[2 * h:2 * h + 1, :] - lam * od_all[2 * h + 1:2 * h + 2, :]
            parts_d.append(seg * lax.rsqrt(jnp.mean(seg * seg, axis=1, keepdims=True) + EPS))
        od_ref[0] = jnp.concatenate(parts_d, axis=1) * dn_ref[...] * (1.0 - lam_init)
        of_ref[0] = jnp.sum(jnp.where(head_cols(of_all.shape), of_all, 0.0), axis=0, keepdims=True)


def _decode_attn(page_table, layer, dq, fq, dk_new, dv_new, fk_new, fv_new, g_new,
                 d_norm, lq1, lk1, lq2, lk2, tri, c_dk, c_dv, c_fk, c_fv, c_flt,
                 pages_per_step, lam_init):
    nb, n_pages = page_table.shape
    G = pages_per_step
    steps = n_pages // G
    per_b = lambda w: pl.BlockSpec((1, 1, w), lambda b, j, pt: (b, 0, 0))
    const = lambda shape: pl.BlockSpec(shape, lambda b, j, pt: (0,) * len(shape))
    in_specs = [per_b(D_WIDTH), per_b(F_WIDTH), per_b(D_WIDTH), per_b(D_WIDTH),
                per_b(F_WIDTH), per_b(F_WIDTH),
                pl.BlockSpec((1, 8, 1), lambda b, j, pt: (b, 0, 0)),
                const((1, D_WIDTH))] + [const((1, D_QK))] * 4 + [const((PAGE_SIZE, PAGE_SIZE))]
    args = [dq, fq, dk_new, dv_new, fk_new, fv_new, g_new, d_norm, lq1, lk1, lq2, lk2, tri]
    for g in range(G):
        def page(rows, w, g=g):
            return pl.BlockSpec((1, 1, rows, w),
                                lambda b, j, pt: (layer, pt[b, j * G + g], 0, 0))
        in_specs += [page(D_ROWS, LANES), page(D_ROWS, LANES),
                     page(F_WIDTH, PAGE_SIZE), page(F_WIDTH, PAGE_SIZE), page(8, PAGE_SIZE)]
        args += [c_dk, c_dv, c_fk, c_fv, c_flt]
    grid_spec = pltpu.PrefetchScalarGridSpec(
        num_scalar_prefetch=1,
        grid=(nb, steps),
        in_specs=in_specs,
        out_specs=[per_b(D_WIDTH), per_b(F_WIDTH)],
        scratch_shapes=[pltpu.VMEM((8, LANES), BF16), pltpu.VMEM((8, F_WIDTH), BF16),
                        pltpu.VMEM((8, 1), F32), pltpu.VMEM((8, 1), F32),
                        pltpu.VMEM((8, LANES), F32),
                        pltpu.VMEM((8, 1), F32), pltpu.VMEM((8, 1), F32),
                        pltpu.VMEM((8, F_WIDTH), F32),
                        pltpu.VMEM((8, 1), F32)])
    return pl.pallas_call(
        functools.partial(_decode_kernel, pages_per_step=G, lam_init=lam_init),
        grid_spec=grid_spec,
        out_shape=[jax.ShapeDtypeStruct((nb, 1, D_WIDTH), F32),
                   jax.ShapeDtypeStruct((nb, 1, F_WIDTH), F32)],
        compiler_params=_cparams(("arbitrary", "arbitrary")),
        name="decode_attn",
    )(page_table, *args)


def _out_proj_kernel(x_ref, hm_ref, od_ref, of_ref, z_ref, w_ref, g_ref, o_ref):
    h = jnp.concatenate([hm_ref[...], od_ref[...], of_ref[...]], axis=1) * _silu(z_ref[...])
    y = _dot(h.astype(BF16), w_ref[...])
    yn = y * lax.rsqrt(jnp.mean(y * y, axis=-1, keepdims=True) + EPS) * g_ref[...]
    o_ref[...] = x_ref[...] + yn


def _out_proj(x2d, hm, od, of, z, w_out, norm_g, tm):
    n = x2d.shape[0]
    row_spec = lambda w: pl.BlockSpec((tm, w), lambda i: (i, 0))
    const = lambda shape: pl.BlockSpec(shape, lambda i: (0,) * len(shape))
    return pl.pallas_call(
        _out_proj_kernel,
        grid=(n // tm,),
        in_specs=[row_spec(D_MODEL), row_spec(M_WIDTH), row_spec(D_WIDTH), row_spec(F_WIDTH),
                  row_spec(MIX_WIDTH), const((MIX_WIDTH, D_MODEL)), const((1, D_MODEL))],
        out_specs=row_spec(D_MODEL),
        out_shape=jax.ShapeDtypeStruct((n, D_MODEL), F32),
        compiler_params=_cparams(("arbitrary",)),
        name="out_proj",
    )(x2d, hm, od, of, z, w_out, norm_g)


def _block_diag(w):
    hh, d, _ = w.shape
    out = jnp.zeros((hh * d, hh * d), w.dtype)
    for h in range(hh):
        out = out.at[h * d:(h + 1) * d, h * d:(h + 1) * d].set(w[h])
    return out


def _layer_weights(l, w_in, b_i, b_f, b_fox, wq_m, wk_m):
    idx = [0]
    for s in SPLITS:
        idx.append(idx[-1] + s)
    cols = lambda k: w_in[l][:, idx[k]:idx[k + 1]]
    u_m, v_m, ig, fg, dq, dk, dv, fq, fk, fv, ffg, z = [cols(k) for k in range(len(SPLITS))]
    w_main = jnp.concatenate([u_m, v_m, dq * (D_QK ** -0.5), dk, dv, fq * (F_DIM ** -0.5), fk, fv, z],
                             axis=1).astype(BF16)
    w_gate_t = jnp.concatenate([ig, fg, ffg, jnp.zeros((D_MODEL, GATE_ROWS - 12), F32)],
                               axis=1).T.astype(BF16)
    gate_bias = jnp.concatenate([b_i[l], b_f[l], b_fox[l], jnp.zeros((GATE_ROWS - 12,), F32)]
                                ).astype(F32).reshape(GATE_ROWS, 1)
    wq_bd = _block_diag(wq_m[l]).astype(BF16)
    wk_bd = (_block_diag(wk_m[l]) * (M_DIM ** -0.5)).astype(BF16)
    return w_main, v_m.T.astype(BF16), w_gate_t, gate_bias, wq_bd, wk_bd


def kernel(x_prompt, x_sample, cache_dk, cache_dv, cache_fk, cache_fv, cache_flogf, state_mconv, state_mC, state_mn, state_mm, page_table, norm_pre, norm_post, w_in, conv_w, conv_b, wq_m, wk_m, b_i, b_f, m_norm, m_skip, lam_q1, lam_k1, lam_q2, lam_k2, d_norm, b_fox, w_out):
    bp, seq, _ = x_prompt.shape
    nb = x_sample.shape[0]
    n_pool = cache_dk.shape[1]
    n_p = bp * seq
    tm_p = 512
    tq, tk = 256, 512
    rc = 64
    pages_per_step = 16
    L = M_CHUNK

    xp = x_prompt.reshape(n_p, D_MODEL)
    ns = LANES
    xs = jnp.pad(x_sample.reshape(nb, D_MODEL), ((0, ns - nb), (0, 0)))
    c_dk = cache_dk.reshape(DEPTH, n_pool, D_ROWS, LANES)
    c_dv = cache_dv.reshape(DEPTH, n_pool, D_ROWS, LANES)
    c_fk = jnp.transpose(cache_fk, (0, 1, 3, 4, 2)).reshape(DEPTH, n_pool, F_WIDTH, PAGE_SIZE)
    c_fv = jnp.transpose(cache_fv, (0, 1, 3, 4, 2)).reshape(DEPTH, n_pool, F_WIDTH, PAGE_SIZE)
    c_flt = jnp.pad(jnp.swapaxes(cache_flogf.astype(F32), 2, 3), ((0, 0), (0, 0), (0, 8 - F_HEADS), (0, 0)))

    tri_p = (lax.broadcasted_iota(jnp.int32, (tm_p, tm_p), 0)
             <= lax.broadcasted_iota(jnp.int32, (tm_p, tm_p), 1)).astype(BF16)
    tri_s = jnp.eye(ns, dtype=BF16)
    tri_page = tri_p[:PAGE_SIZE, :PAGE_SIZE]

    zero_hist = jnp.zeros((bp, 8, M_WIDTH), F32)
    zero_c = jnp.zeros((bp, M_HEADS, M_DIM, M_DIM), F32)
    zero_n = jnp.zeros((bp, M_HEADS, M_DIM, LANES), F32)
    zero_m = jnp.zeros((bp, 1, LANES), F32)

    new_p, new_s = [], []
    for l in range(DEPTH):
        lam_init = 0.8 - 0.6 * math.exp(-0.3 * l)
        w_main, w_vm_t, w_gate_t, gate_bias, wq_bd, wk_bd = _layer_weights(l, w_in, b_i, b_f, b_fox, wq_m, wk_m)
        w_out_l = w_out[l].astype(BF16)
        g_pre = norm_pre[l].reshape(1, D_MODEL)
        g_post = norm_post[l].reshape(1, D_MODEL)
        cw, cb = conv_w[l], conv_b[l].reshape(1, M_WIDTH)
        mnorm, mskip = m_norm[l].reshape(1, M_WIDTH), m_skip[l].reshape(1, M_WIDTH)
        dn = d_norm[l].reshape(1, D_WIDTH)
        lams = [a[l].reshape(1, D_QK).astype(F32) for a in (lam_q1, lam_k1, lam_q2, lam_k2)]

        (um, vm, dq, dk32, dk16, dv32, dv16, fq, fk32, fk16, fv32, fv16, z, gates, fkt, fvt) = _in_proj(
            xp, g_pre, w_main, w_vm_t, w_gate_t, gate_bias, tri_p, tm_p, seq // tm_p)
        gates_seq = gates.reshape(GATE_ROWS, bp, seq).transpose(1, 0, 2)
        hm, ct_fin, n_fin, m_fin = _mlstm(um.reshape(bp, seq, M_WIDTH), vm, gates_seq,
                                          zero_hist, zero_c, zero_n, zero_m, tri_page,
                                          cw, cb, wq_bd, wk_bd, mnorm, mskip)
        c_fin, n_fin = jnp.swapaxes(ct_fin, 2, 3), n_fin[..., 0]
        hm = hm.reshape(n_p, M_WIDTH)
        od = _prompt_attn(dq, dk16, dv16, [dn] + lams, bp, seq, tq, tk, rc, False, lam_init)
        of = _prompt_attn(fq, fk16, fv16, [gates], bp, seq, tq, tk, rc, True, lam_init)
        xp = _out_proj(xp, hm, od, of, z, w_out_l, g_post, tm_p)
        new_p.append((dk32.reshape(bp, seq, D_HEADS, 2 * D_QK), dv32.reshape(bp, seq, D_HEADS, D_V),
                      fkt.reshape(bp, F_HEADS, F_DIM, seq).transpose(0, 3, 1, 2),
                      fvt.reshape(bp, F_HEADS, F_DIM, seq).transpose(0, 3, 1, 2),
                      gates[8:12].T.reshape(bp, seq, F_HEADS),
                      um.reshape(bp, seq, M_WIDTH)[:, seq - (CONV_W - 1):],
                      c_fin, n_fin, m_fin[:, 0, :M_HEADS]))

        outs_s = _in_proj(xs, g_pre, w_main, w_vm_t, w_gate_t, gate_bias, tri_s, ns, 1)
        (um, vm, dq, dk32, _, dv32, _, fq, fk32, _, fv32, _, z, gates) = [
            a[:, :nb] if k in (1, 13) else a[:nb * (a.shape[0] // ns)] for k, a in enumerate(outs_s[:14])]
        z = outs_s[12]
        dk32, dv32 = dk32.reshape(nb, D_WIDTH), dv32.reshape(nb, D_WIDTH)
        pad_tok = lambda a: jnp.pad(a[:, None, :], ((0, 0), (0, L - 1), (0, 0)))
        noop = jnp.where(jnp.arange(GATE_ROWS) < M_HEADS, NEG_INF, 0.0).astype(F32)
        gates_pad = jnp.concatenate(
            [gates.T[:, :, None], jnp.broadcast_to(noop[None, :, None], (nb, GATE_ROWS, L - 1))], axis=2)
        hist = state_mconv[l].astype(F32)
        hist8 = jnp.pad(hist, ((0, 0), (8 - (CONV_W - 1), 0), (0, 0)))
        m0 = jnp.pad(state_mm[l].astype(F32), ((0, 0), (0, LANES - M_HEADS)))[:, None, :]
        vm_pad = jnp.pad(vm[:, :, None], ((0, 0), (0, 0), (0, L - 1))).reshape(M_WIDTH, nb * L)
        n0 = jnp.broadcast_to(state_mn[l].astype(F32)[..., None], (nb, M_HEADS, M_DIM, LANES))
        hm, ct_fin, n_fin, m_fin = _mlstm(pad_tok(um), vm_pad, gates_pad, hist8,
                                          jnp.swapaxes(state_mC[l].astype(F32), 2, 3), n0, m0, tri_page,
                                          cw, cb, wq_bd, wk_bd, mnorm, mskip)
        c_fin, n_fin = jnp.swapaxes(ct_fin, 2, 3), n_fin[..., 0]
        hm = hm[:, 0]
        flog_new = gates[8:12].T
        g_new = jnp.pad(flog_new, ((0, 0), (0, 8 - F_HEADS)))[:, :, None]
        od, of = _decode_attn(page_table, l, dq[:, None, :], fq[:, None, :], dk32[:, None, :],
                              dv32[:, None, :], fk32[:, None, :], fv32[:, None, :], g_new,
                              dn, *lams, tri_page, c_dk, c_dv, c_fk, c_fv, c_flt, pages_per_step, lam_init)
        pad_rows = lambda a: jnp.pad(a, ((0, ns - nb), (0, 0)))
        xs = _out_proj(xs, pad_rows(hm), pad_rows(od[:, 0]), pad_rows(of[:, 0]), z, w_out_l, g_post, ns)
        new_s.append((dk32.reshape(nb, 1, D_HEADS, 2 * D_QK), dv32.reshape(nb, 1, D_HEADS, D_V),
                      fk32.reshape(nb, 1, F_HEADS, F_DIM), fv32.reshape(nb, 1, F_HEADS, F_DIM),
                      flog_new.reshape(nb, 1, F_HEADS),
                      jnp.concatenate([hist[:, 1:], um[:, None, :]], axis=1),
                      c_fin, n_fin, m_fin[:, 0, :M_HEADS]))

    stack = lambda states, k: jnp.stack([st[k] for st in states])
    return ((xp.reshape(bp, seq, D_MODEL), xs[:nb].reshape(nb, 1, D_MODEL))
            + tuple(stack(new_p, k) for k in range(9))
            + tuple(stack(new_s, k) for k in range(9)))
```
